```python
import math
import jax
import jax.numpy as jnp
from jax import lax
import numpy as np


D_MODEL = 1024
BATCH = 4
SEQ = 4096
DEPTH = 4

GRID_W = 64
CTX_LEN = 256
N_BRANCH = 3
BR_W = D_MODEL // 2
A_GROUPS = 4
A_GW = BR_W // A_GROUPS
A_CHUNK = 128
B_HD = 64
B_VD = 2 * B_HD
B_HEADS = BR_W // (2 * B_HD)
ATTN_BLOCK = 128
ROPE_BASE = 10000.0
C_HD = 64
C_HEADS = BR_W // C_HD
N_DIR = 2
W_LORA = 64
A_LORA = 64
NORM_EPS = 1e-6
LN_EPS = 1e-5
GN_EPS = 64e-5
STATE_SIZES = (BR_W, BR_W, BR_W, BR_W, N_DIR * W_LORA, N_DIR * A_LORA)
OUT_SIZES = (BR_W,) * 7 + (N_BRANCH * D_MODEL,)
STATE_COLS = 4 * BR_W + N_DIR * (W_LORA + A_LORA)
OUT_COLS = 7 * BR_W + N_BRANCH * D_MODEL
IN_COLS = STATE_COLS + OUT_COLS

kernel_name = 'hybrid_gmlp_diffattn_rwkv7_flow_block'


def rms_norm(x, g):
    xf = x.astype(jnp.float32)
    y = xf * lax.rsqrt(jnp.mean(xf * xf, axis=-1, keepdims=True) + NORM_EPS)
    return (y * g).astype(x.dtype)


def layer_norm(x, g, b):
    xf = x.astype(jnp.float32)
    mu = jnp.mean(xf, axis=-1, keepdims=True)
    var = jnp.mean(jnp.square(xf - mu), axis=-1, keepdims=True)
    return ((xf - mu) * lax.rsqrt(var + LN_EPS) * g + b).astype(x.dtype)


def split_cols(t, sizes):
    return jnp.split(t, np.cumsum(sizes)[:-1].tolist(), axis=-1)


def rope_table(pos):
    nf = B_HD // 4
    inv = ROPE_BASE ** (-jnp.arange(nf, dtype=jnp.float32) / nf)
    ang = pos.astype(jnp.float32)[:, None] * inv[None, :]
    return jnp.cos(ang)[:, None, None, :], jnp.sin(ang)[:, None, None, :]


def rotate(x, cos, sin):
    x1, x2 = jnp.split(x, 2, axis=-1)
    cos = cos.astype(x.dtype)
    sin = sin.astype(x.dtype)
    return jnp.concatenate([x1 * cos - x2 * sin, x1 * sin + x2 * cos], axis=-1)


def axial_rope(x, rope):
    (rc, rs), (cc, cs) = rope
    x_row, x_col = jnp.split(x, 2, axis=-1)
    return jnp.concatenate([rotate(x_row, rc, rs), rotate(x_col, cc, cs)], axis=-1)


def token_shift(x, taps):
    xp = jnp.pad(x, ((0, 0), (1, 1), (0, 0)))
    return xp[:, :-2] * taps[0] + xp[:, 1:-1] * taps[1] + xp[:, 2:] * taps[2]


def dir_stack(t):
    t = jnp.moveaxis(t, 2, 0)
    return jnp.stack([t[0], jnp.flip(t[1], axis=1)])


def dir_share(t):
    return jnp.stack([t, jnp.flip(t, axis=1)])


def state_side(ps, p, rope):
    b_, t_ = ps.shape[:2]
    dk, dv, kr, vr, wl, al = split_cols(ps, STATE_SIZES)
    k_att = rms_norm(dk.reshape(b_, t_, B_HEADS, 2, B_HD), p['d_knorm'])
    if rope is not None:
        k_att = axial_rope(k_att, rope)
    v_att = dv.reshape(b_, t_, B_HEADS, B_VD)
    k = token_shift(kr, p['r_conv'][1])
    v = token_shift(vr, p['r_conv'][2])
    wl = wl.reshape(b_, t_, N_DIR, W_LORA)
    al = al.reshape(b_, t_, N_DIR, A_LORA)
    w_log = (p['r_w0'] + jnp.einsum('btdr,drc->btdc', jnp.tanh(wl), p['r_w2'])).astype(jnp.float32)
    decay = jnp.exp(-jnp.exp(-jax.nn.softplus(-w_log) - 0.5))
    a = jax.nn.sigmoid(p['r_a0'] + jnp.einsum('btdr,drc->btdc', al, p['r_a2']))
    kk = (k * p['r_kk']).reshape(b_, t_, C_HEADS, C_HD).astype(jnp.float32)
    kk = kk / jnp.maximum(jnp.sqrt(jnp.sum(kk * kk, axis=-1, keepdims=True)), 1e-12)
    k_dir = k[:, :, None, :] * (1.0 + (a - 1.0) * p['r_ka'])
    heads = lambda z: z.reshape(b_, t_, N_DIR, C_HEADS, C_HD)
    k_h = k.reshape(b_, t_, C_HEADS, C_HD)
    v_h = v.reshape(b_, t_, C_HEADS, C_HD)
    return dict(k_att=k_att, v_att=v_att, k_h=k_h, v_h=v_h,
                kk=dir_share(kk), a=dir_stack(heads(a)), w=dir_stack(heads(decay)),
                k=dir_stack(heads(k_dir)), v=dir_share(v_h))


def out_side(po, p, rope):
    b_, t_ = po.shape[:2]
    dq, r, u, va, za, zb, zc, gl = split_cols(po, OUT_SIZES)
    q = rms_norm(dq.reshape(b_, t_, B_HEADS, 2, B_HD), p['d_qnorm'])
    if rope is not None:
        q = axial_rope(q, rope)
    r_h = token_shift(r, p['r_conv'][0]).reshape(b_, t_, C_HEADS, C_HD)
    return dict(q=q, r_h=r_h, u=jax.nn.gelu(u), va=jax.nn.gelu(va), za=za, zb=zb, zc=zc, gl=gl)


def chunk_gmlp(u, v, p):
    b_, t_, _ = v.shape
    v = layer_norm(v, p['a_ln_g'], p['a_ln_b'])
    vc = v.reshape(b_, t_ // A_CHUNK, A_CHUNK, A_GROUPS, A_GW)
    s = jnp.einsum('gpq,bnqgc->bnpgc', p['a_ws'], vc) + jnp.swapaxes(p['a_bs'], 0, 1)[None, None, :, :, None]
    return u * s.reshape(b_, t_, BR_W)


def diff_lambda(lam_p, lam_init):
    lp = lam_p.astype(jnp.float32)
    return jnp.exp(jnp.sum(lp[0] * lp[1])) - jnp.exp(jnp.sum(lp[2] * lp[3])) + lam_init


def diff_attention(q, k, v, lam):
    b_, t_ = q.shape[:2]
    nb = t_ // ATTN_BLOCK
    qb = jnp.moveaxis(q.reshape(b_, nb, ATTN_BLOCK, B_HEADS, 2, B_HD), 1, 0)
    scale = B_HD ** -0.5

    def one_block(qi):
        s = jnp.einsum('bqhjd,bkhjd->bhjqk', qi, k).astype(jnp.float32) * scale
        pr = jax.nn.softmax(s, axis=-1)
        pd = pr[:, :, 0] - lam * pr[:, :, 1]
        return jnp.einsum('bhqk,bkhe->bqhe', pd.astype(v.dtype), v)

    out = lax.map(one_block, qb)
    return jnp.moveaxis(out, 0, 1).reshape(b_, t_, B_HEADS, B_VD)


def rwkv_update(s, kk, a, w, k, v):
    sa = jnp.einsum('dbhvk,dbhk->dbhv', s, -kk)
    return s * w[..., None, :] + sa[..., :, None] * (kk * a)[..., None, :] + v[..., :, None] * k[..., None, :]


def rwkv_scan(s0, st, r=None):
    xs = tuple(jnp.moveaxis(st[n].astype(jnp.float32), 2, 0) for n in ('kk', 'a', 'w', 'k', 'v'))
    if r is None:
        def step_state(s, inp):
            return rwkv_update(s, *inp), None
        s_fin, _ = lax.scan(step_state, s0, xs)
        return s_fin, None

    def step(s, inp):
        s = rwkv_update(s, *inp[:5])
        return s, jnp.einsum('dbhvk,dbhk->dbhv', s, inp[5])

    s_fin, ys = lax.scan(step, s0, xs + (jnp.moveaxis(r.astype(jnp.float32), 2, 0),))
    return s_fin, jnp.moveaxis(ys, 0, 2)


def rwkv_readout(yd, r_h, k_h, v_h, p):
    b_, t_ = r_h.shape[:2]
    y = yd[0] + jnp.flip(yd[1], axis=1)
    mu = jnp.mean(y, axis=-1, keepdims=True)
    var = jnp.mean(jnp.square(y - mu), axis=-1, keepdims=True)
    y = ((y - mu) * lax.rsqrt(var + GN_EPS)).reshape(b_, t_, BR_W) * p['r_ln_g'] + p['r_ln_b']
    bonus = jnp.sum(r_h * k_h * p['r_rk'], axis=-1, keepdims=True) * v_h
    return (y + bonus.reshape(b_, t_, BR_W).astype(jnp.float32)).astype(r_h.dtype)


def merge(ya, yb, yc, gl, w_br, w_out):
    b_, t_ = ya.shape[:2]
    ys = jnp.stack([ya, yb, yc], axis=2)
    up = jnp.einsum('btic,icd->btid', ys, w_br)
    g = jax.nn.sigmoid(gl.reshape(b_, t_, N_BRANCH, D_MODEL))
    return jnp.einsum('btd,de->bte', jnp.sum(g * up, axis=2), w_out)


def mixer_out(so, st, k_all, v_all, s0, lam, lam_init, p):
    b_, t_ = so['u'].shape[:2]
    ya = chunk_gmlp(so['u'], so['va'], p) * jax.nn.silu(so['za'])
    att = diff_attention(so['q'], k_all, v_all, lam)
    att = rms_norm(att, p['d_subln_g']) * (1.0 - lam_init)
    yb = att.reshape(b_, t_, BR_W) * jax.nn.silu(so['zb'])
    s_fin, yd = rwkv_scan(s0, st, dir_share(so['r_h']))
    yc = rwkv_readout(yd, so['r_h'], st['k_h'], st['v_h'], p) * jax.nn.silu(so['zc'])
    return merge(ya, yb, yc, so['gl'], p['w_br'], p['w_out']), s_fin


def hybrid_layer(x, xc, c_act, cc_act, rope, lam_init, p, update_ctx):
    d = D_MODEL
    shift, scale, gate = jnp.split((c_act @ p['w_mod'] + p['b_mod'])[:, None, :], 3, axis=-1)
    n_mod_c = 3 if update_ctx else 2
    mod_c = cc_act @ p['w_mod'][:, :n_mod_c * d] + p['b_mod'][:n_mod_c * d]
    h = rms_norm(x, p['norm_g']) * (1.0 + scale) + shift
    hc = rms_norm(xc, p['norm_g']) * (1.0 + mod_c[d:2 * d]) + mod_c[:d]
    lam = diff_lambda(p['d_lam'], lam_init)
    s_zero = jnp.zeros((N_DIR, xc.shape[0], C_HEADS, C_HD, C_HD), jnp.float32)

    if update_ctx:
        pc = hc @ p['w_in']
        st_c = state_side(pc[..., :STATE_COLS], p, None)
        so_c = out_side(pc[..., STATE_COLS:], p, None)
        out_c, s_ctx = mixer_out(so_c, st_c, st_c['k_att'], st_c['v_att'], s_zero, lam, lam_init, p)
        xc_next = xc + mod_c[2 * d:] * out_c
    else:
        st_c = state_side(hc @ p['w_in'][:, :STATE_COLS], p, None)
        s_ctx, _ = rwkv_scan(s_zero, st_c)
        xc_next = None

    pl = h @ p['w_in']
    st = state_side(pl[..., :STATE_COLS], p, rope)
    so = out_side(pl[..., STATE_COLS:], p, rope)
    k_all = jnp.concatenate([st['k_att'], st_c['k_att']], axis=1)
    v_all = jnp.concatenate([st['v_att'], st_c['v_att']], axis=1)
    out, _ = mixer_out(so, st, k_all, v_all, s_ctx, lam, lam_init, p)
    return x + gate * out, xc_next


def setup_inputs(seed: int = 0) -> dict:
    key = jax.random.key(seed)
    keys = jax.random.split(key, 28)
    nrm = lambda i, shape, s=1.0: jax.random.normal(keys[i], shape, jnp.float32) * s
    L, D = DEPTH, D_MODEL
    taps = jnp.array([0.2, 0.6, 0.2], jnp.float32)[None, None, :, None]
    return {
        'x': nrm(0, (BATCH, SEQ, D)),
        'c': nrm(1, (BATCH, D)),
        'ctx': nrm(2, (BATCH, CTX_LEN, D)),
        'c_ctx': nrm(3, (D,)),
        'w_mod': nrm(4, (L, D, 3 * D), 0.5 * D ** -0.5),
        'b_mod': nrm(5, (L, 3 * D), 0.02),
        'norm_g': 1.0 + nrm(6, (L, D), 0.02),
        'w_in': nrm(7, (L, D, IN_COLS), D ** -0.5),
        'a_ln_g': 1.0 + nrm(8, (L, BR_W), 0.02),
        'a_ln_b': nrm(9, (L, BR_W), 0.02),
        'a_ws': nrm(10, (L, A_GROUPS, A_CHUNK, A_CHUNK), A_CHUNK ** -0.5),
        'a_bs': 1.0 + nrm(11, (L, A_GROUPS, A_CHUNK), 0.02),
        'd_qnorm': 1.0 + nrm(12, (L, B_HD), 0.02),
        'd_knorm': 1.0 + nrm(13, (L, B_HD), 0.02),
        'd_lam': nrm(14, (L, 4, B_HD), 0.1),
        'd_subln_g': 1.0 + nrm(15, (L, B_VD), 0.02),
        'r_conv': taps + nrm(16, (L, 3, 3, BR_W), 0.05),
        'r_w0': jnp.linspace(-6.0, -1.0, BR_W, dtype=jnp.float32) + nrm(17, (L, N_DIR, BR_W), 0.1),
        'r_w2': nrm(18, (L, N_DIR, W_LORA, BR_W), 0.3 * W_LORA ** -0.5),
        'r_a0': nrm(19, (L, N_DIR, BR_W), 0.1),
        'r_a2': nrm(20, (L, N_DIR, A_LORA, BR_W), 0.3 * A_LORA ** -0.5),
        'r_kk': 0.85 + nrm(21, (L, BR_W), 0.02),
        'r_ka': 1.0 + nrm(22, (L, BR_W), 0.02),
        'r_rk': nrm(23, (L, C_HEADS, C_HD), 0.1),
        'r_ln_g': 1.0 + nrm(24, (L, BR_W), 0.02),
        'r_ln_b': nrm(25, (L, BR_W), 0.02),
        'w_br': nrm(26, (L, N_BRANCH, BR_W, D), BR_W ** -0.5),
        'w_out': nrm(27, (L, D, D), D ** -0.5),
    }


def reference(x, c, ctx, c_ctx, w_mod, b_mod, norm_g, w_in, a_ln_g, a_ln_b, a_ws, a_bs,
              d_qnorm, d_knorm, d_lam, d_subln_g, r_conv, r_w0, r_w2, r_a0, r_a2,
              r_kk, r_ka, r_rk, r_ln_g, r_ln_b, w_br, w_out):
    n_tok = x.shape[1]
    rows = n_tok // GRID_W
    row = jnp.broadcast_to(jnp.arange(rows)[:, None], (rows, GRID_W)).reshape(-1)
    col = jnp.broadcast_to(jnp.arange(GRID_W)[None, :], (rows, GRID_W)).reshape(-1)
    rope = (rope_table(row), rope_table(col))
    c_act = jax.nn.silu(c)
    cc_act = jax.nn.silu(c_ctx)
    xc = ctx
    for l in range(DEPTH):
        p = dict(w_mod=w_mod[l], b_mod=b_mod[l], norm_g=norm_g[l], w_in=w_in[l],
                 a_ln_g=a_ln_g[l], a_ln_b=a_ln_b[l], a_ws=a_ws[l], a_bs=a_bs[l],
                 d_qnorm=d_qnorm[l], d_knorm=d_knorm[l], d_lam=d_lam[l], d_subln_g=d_subln_g[l],
                 r_conv=r_conv[l], r_w0=r_w0[l], r_w2=r_w2[l], r_a0=r_a0[l], r_a2=r_a2[l],
                 r_kk=r_kk[l], r_ka=r_ka[l], r_rk=r_rk[l], r_ln_g=r_ln_g[l], r_ln_b=r_ln_b[l],
                 w_br=w_br[l], w_out=w_out[l])
        lam_init = 0.8 - 0.6 * math.exp(-0.3 * l)
        x, xc = hybrid_layer(x, xc, c_act, cc_act, rope, lam_init, p, l < DEPTH - 1)
    return x
```

```python
import functools
import math

import jax
import jax.numpy as jnp
from jax import lax
from jax.experimental import pallas as pl
from jax.experimental.pallas import tpu as pltpu

D_MODEL = 1024
GRID_W = 64
N_BRANCH = 3
BR_W = D_MODEL // 2
A_GROUPS = 4
A_GW = BR_W // A_GROUPS
A_CHUNK = 128
B_HD = 64
B_VD = 2 * B_HD
B_HEADS = BR_W // B_VD
ROPE_BASE = 10000.0
C_HD = 64
C_HEADS = BR_W // C_HD
N_DIR = 2
W_LORA = 64
A_LORA = 64
NORM_EPS = 1e-6
LN_EPS = 1e-5
GN_EPS = 64e-5
STATE_SIZES = (BR_W, BR_W, BR_W, BR_W, N_DIR * W_LORA, N_DIR * A_LORA)
OUT_SIZES = (BR_W,) * 7 + (N_BRANCH * D_MODEL,)
STATE_COLS = sum(STATE_SIZES)
OUT_COLS = sum(OUT_SIZES)

SCAN_CHUNK = 64
VMEM_LIMIT_BYTES = 48 * 1024 * 1024
LOG2E = 1.4426950408889634

_HI = lax.Precision.HIGHEST


def _tile(n, pref):
    if n <= pref:
        return n
    t = pref
    while n % t:
        t //= 2
    return t


def _mm_kernel(a_ref, b_ref, o_ref):
    o_ref[...] = jnp.dot(a_ref[...], b_ref[...], preferred_element_type=jnp.float32)


def _matmul(a, b, tm=512, tn=1280):
    a = a.astype(jnp.bfloat16)
    b = b.astype(jnp.bfloat16)
    m, k = a.shape
    n = b.shape[1]
    tm = _tile(m, tm)
    tn = n if n <= tn else (tn if n % tn == 0 else _tile(n, 1024))
    return pl.pallas_call(
        _mm_kernel,
        grid=(m // tm, n // tn),
        in_specs=[pl.BlockSpec((tm, k), lambda i, j: (i, 0)),
                  pl.BlockSpec((k, tn), lambda i, j: (0, j))],
        out_specs=pl.BlockSpec((tm, tn), lambda i, j: (i, j)),
        out_shape=jax.ShapeDtypeStruct((m, n), jnp.float32),
        compiler_params=pltpu.CompilerParams(
            dimension_semantics=("parallel", "parallel"), vmem_limit_bytes=VMEM_LIMIT_BYTES),
        name="matmul",
    )(a, b)


def _attn_kernel(lam_ref, q_ref, *refs, chunks):
    n_src = len(chunks)
    k_refs = refs[:n_src]
    v_refs = refs[n_src:2 * n_src]
    o_ref = refs[2 * n_src]
    q = q_ref[0]
    tq = q.shape[0]
    lane = lax.broadcasted_iota(jnp.int32, (1, B_VD), 1)
    zero = jnp.zeros_like(q)
    q1 = jnp.where(lane < B_HD, q, zero)
    q2 = jnp.where(lane >= B_HD, q, zero)

    def update(carry, k, v):
        new = []
        for qj, (m, l, acc) in zip((q1, q2), carry):
            s = lax.dot_general(qj, k, (((1,), (1,)), ((), ())), preferred_element_type=jnp.float32)
            m_new = jnp.maximum(m, jnp.max(s, axis=-1, keepdims=True))
            alpha = jnp.exp2(m - m_new)
            p = jnp.exp2(s - m_new)
            l_new = alpha * l + jnp.sum(p, axis=-1, keepdims=True)
            acc_new = alpha * acc + jnp.dot(p.astype(v.dtype), v, preferred_element_type=jnp.float32)
            new.append((m_new, l_new, acc_new))
        return tuple(new)

    init = tuple((jnp.full((tq, 1), -jnp.inf, jnp.float32), jnp.zeros((tq, 1), jnp.float32),
                  jnp.zeros((tq, B_VD), jnp.float32)) for _ in range(2))
    carry = init
    for k_ref, v_ref, (n_chunk, tk) in zip(k_refs, v_refs, chunks):
        def body(i, c, k_ref=k_ref, v_ref=v_ref, tk=tk):
            off = pl.multiple_of(i * tk, tk)
            return update(c, k_ref[0, pl.ds(off, tk), :], v_ref[0, pl.ds(off, tk), :])
        carry = lax.fori_loop(0, n_chunk, body, carry)
    (_, l1, acc1), (_, l2, acc2) = carry
    lam = lam_ref[0]
    o_ref[0] = acc1 / l1 - lam * (acc2 / l2)


def _diff_attention(q, ks, vs, lam):
    b_, t_, _ = q.shape
    tq = _tile(t_, 256)
    chunks = tuple((k.shape[1] // _tile(k.shape[1], 512), _tile(k.shape[1], 512)) for k in ks)
    kv_specs = [pl.BlockSpec((1, k.shape[1], B_VD), lambda b, h, i: (b, 0, h)) for k in ks]
    return pl.pallas_call(
        functools.partial(_attn_kernel, chunks=chunks),
        grid=(b_, B_HEADS, t_ // tq),
        in_specs=[pl.BlockSpec(memory_space=pltpu.SMEM),
                  pl.BlockSpec((1, tq, B_VD), lambda b, h, i: (b, i, h))] + kv_specs + kv_specs,
        out_specs=pl.BlockSpec((1, tq, B_VD), lambda b, h, i: (b, i, h)),
        out_shape=jax.ShapeDtypeStruct((b_, t_, BR_W), jnp.float32),
        compiler_params=pltpu.CompilerParams(
            dimension_semantics=("parallel", "parallel", "parallel"), vmem_limit_bytes=VMEM_LIMIT_BYTES),
        name="diff_attention",
    )(lam.reshape(1).astype(jnp.float32), q, *ks, *vs)


def _gmlp_kernel(v_ref, ws_ref, bias_ref, o_ref):
    rows = v_ref.shape[0]
    for n in range(rows // A_CHUNK):
        r0 = n * A_CHUNK
        for g in range(A_GROUPS):
            c0 = g * A_GW
            s = jnp.dot(ws_ref[g], v_ref[r0:r0 + A_CHUNK, c0:c0 + A_GW], preferred_element_type=jnp.float32)
            o_ref[r0:r0 + A_CHUNK, c0:c0 + A_GW] = s + bias_ref[:, c0:c0 + A_GW]


def _gmlp_mix(vn, ws, bs):
    n = vn.shape[0]
    rows = _tile(n, 512)
    bias = jnp.repeat(jnp.swapaxes(bs, 0, 1), A_GW, axis=1).astype(jnp.float32)
    return pl.pallas_call(
        _gmlp_kernel,
        grid=(n // rows,),
        in_specs=[pl.BlockSpec((rows, BR_W), lambda i: (i, 0)),
                  pl.BlockSpec((A_GROUPS, A_CHUNK, A_CHUNK), lambda i: (0, 0, 0)),
                  pl.BlockSpec((A_CHUNK, BR_W), lambda i: (0, 0))],
        out_specs=pl.BlockSpec((rows, BR_W), lambda i: (i, 0)),
        out_shape=jax.ShapeDtypeStruct((n, BR_W), jnp.float32),
        compiler_params=pltpu.CompilerParams(
            dimension_semantics=("parallel",), vmem_limit_bytes=VMEM_LIMIT_BYTES),
        name="gmlp_mix",
    )(vn.astype(jnp.bfloat16), ws.astype(jnp.bfloat16), bias)


def _dot(a, b):
    return jnp.dot(a, b, preferred_element_type=jnp.float32, precision=_HI)


def _dot_tn(a, b):
    return lax.dot_general(a, b, (((0,), (0,)), ((), ())), preferred_element_type=jnp.float32, precision=_HI)


def _dot_nt(a, b):
    return lax.dot_general(a, b, (((1,), (1,)), ((), ())), preferred_element_type=jnp.float32, precision=_HI)


def _scan_kernel(lw_ref, a_ref, kd_ref, kk_ref, v_ref, r_ref, h0_ref, y_ref, h_ref):
    d = pl.program_id(0)
    c = pl.program_id(2)

    @pl.when(c == 0)
    def _():
        h_ref[...] = h0_ref[...]

    n_tok = lw_ref.shape[1]
    row = lax.broadcasted_iota(jnp.int32, (n_tok, n_tok), 0)
    col = lax.broadcasted_iota(jnp.int32, (n_tok, n_tok), 1)
    sgn = 1 - 2 * d
    diff = (row - col) * sgn
    strict = diff > 0
    incl = diff >= 0
    incl_f = incl.astype(jnp.float32)
    eye_f = (row == col).astype(jnp.float32)
    eye_n = (lax.broadcasted_iota(jnp.int32, (C_HD, C_HD), 0)
             == lax.broadcasted_iota(jnp.int32, (C_HD, C_HD), 1)).astype(jnp.float32)

    for h in range(C_HEADS):
        lw = lw_ref[h]
        cum = _dot(incl_f, lw)
        p_in = jnp.exp(cum)
        p_ex = jnp.exp(cum - lw)
        p_inv = jnp.exp(-cum)
        p_tot = jnp.exp(jnp.sum(lw, axis=0, keepdims=True))
        kk = kk_ref[h]
        at = -kk * p_ex
        bt = kk * a_ref[h] * p_inv
        kt = kd_ref[h] * p_inv
        rt = r_ref[h] * p_in
        v = v_ref[h]
        h0 = h_ref[h]

        a_ab = jnp.where(strict, _dot_nt(at, bt), 0.0)
        a_ak = jnp.where(strict, _dot_nt(at, kt), 0.0)
        a_rb = jnp.where(incl, _dot_nt(rt, bt), 0.0)
        a_rk = jnp.where(incl, _dot_nt(rt, kt), 0.0)

        tinv = eye_f + a_ab
        apow = a_ab
        span = 2
        while span < n_tok:
            apow = _dot(apow, apow)
            tinv = tinv + _dot(apow, tinv)
            span *= 2

        z = _dot(at, h0) + _dot(a_ak, v)
        u = _dot(tinv, z)
        y_ref[h] = _dot(rt, h0) + _dot(a_rb, u) + _dot(a_rk, v)
        h_new = h0 + _dot_tn(bt, u) + _dot_tn(kt, v)
        h_ref[h] = _dot(eye_n * p_tot, h_new)


def _rwkv_scan(lw, a, kd, kk, v, r, h0):
    _, b_, h_, t_, n_ = lw.shape
    ck = _tile(t_, SCAN_CHUNK)
    nc = t_ // ck

    def tok_dir(d, b, c):
        return (d, b, 0, c + d * (nc - 1 - 2 * c), 0)

    def tok_shared(d, b, c):
        return (b, 0, c + d * (nc - 1 - 2 * c), 0)

    dir_spec = pl.BlockSpec((None, None, h_, ck, n_), tok_dir)
    shared_spec = pl.BlockSpec((None, h_, ck, n_), tok_shared)
    state_spec = pl.BlockSpec((None, None, h_, n_, n_), lambda d, b, c: (d, b, 0, 0, 0))
    return pl.pallas_call(
        _scan_kernel,
        grid=(N_DIR, b_, nc),
        in_specs=[dir_spec, dir_spec, dir_spec, shared_spec, shared_spec, shared_spec, state_spec],
        out_specs=[dir_spec, state_spec],
        out_shape=[jax.ShapeDtypeStruct((N_DIR, b_, h_, t_, n_), jnp.float32),
                   jax.ShapeDtypeStruct((N_DIR, b_, h_, n_, n_), jnp.float32)],
        compiler_params=pltpu.CompilerParams(
            dimension_semantics=("parallel", "parallel", "arbitrary"), vmem_limit_bytes=VMEM_LIMIT_BYTES),
        name="rwkv_scan",
    )(lw, a, kd, kk, v, r, h0)


def _rms_norm(x, g):
    return x * lax.rsqrt(jnp.mean(x * x, axis=-1, keepdims=True) + NORM_EPS) * g


def _layer_norm(x, g, b):
    mu = jnp.mean(x, axis=-1, keepdims=True)
    var = jnp.mean(jnp.square(x - mu), axis=-1, keepdims=True)
    return (x - mu) * lax.rsqrt(var + LN_EPS) * g + b


def _split_cols(t, sizes):
    out, o = [], 0
    for s in sizes:
        out.append(t[..., o:o + s])
        o += s
    return out


def _rope_table(pos):
    nf = B_HD // 4
    inv = ROPE_BASE ** (-jnp.arange(nf, dtype=jnp.float32) / nf)
    ang = pos.astype(jnp.float32)[:, None] * inv[None, :]
    return jnp.cos(ang)[:, None, None, :], jnp.sin(ang)[:, None, None, :]


def _rotate(x, cos, sin):
    x1, x2 = jnp.split(x, 2, axis=-1)
    return jnp.concatenate([x1 * cos - x2 * sin, x1 * sin + x2 * cos], axis=-1)


def _axial_rope(x, rope):
    (rc, rs), (cc, cs) = rope
    x_row, x_col = jnp.split(x, 2, axis=-1)
    return jnp.concatenate([_rotate(x_row, rc, rs), _rotate(x_col, cc, cs)], axis=-1)


def _token_shift(x, taps):
    xp = jnp.pad(x, ((0, 0), (1, 1), (0, 0)))
    return xp[:, :-2] * taps[0] + xp[:, 1:-1] * taps[1] + xp[:, 2:] * taps[2]


def _heads_major(t):
    s = t.shape
    return jnp.swapaxes(t.reshape(s[:-1] + (C_HEADS, C_HD)), -3, -2)


def _state_side(ps, p, rope):
    b_, t_ = ps.shape[:2]
    dk, dv, kr, vr, wl, al = _split_cols(ps, STATE_SIZES)
    k_att = _rms_norm(dk.reshape(b_, t_, B_HEADS, 2, B_HD), p['d_knorm'])
    if rope is not None:
        k_att = _axial_rope(k_att, rope)
    k_att = k_att.reshape(b_, t_, BR_W).astype(jnp.bfloat16)
    v_att = dv.astype(jnp.bfloat16)
    k = _token_shift(kr, p['r_conv'][1])
    v = _token_shift(vr, p['r_conv'][2])
    wl = jnp.tanh(wl).reshape(b_ * t_, N_DIR, W_LORA)
    al = al.reshape(b_ * t_, N_DIR, A_LORA)
    w_lora = jnp.stack([_matmul(wl[:, d], p['r_w2'][d]) for d in range(N_DIR)]).reshape(N_DIR, b_, t_, BR_W)
    a_lora = jnp.stack([_matmul(al[:, d], p['r_a2'][d]) for d in range(N_DIR)]).reshape(N_DIR, b_, t_, BR_W)
    w_log = p['r_w0'][:, None, None, :] + w_lora
    log_decay = -jnp.exp(-jax.nn.softplus(-w_log) - 0.5)
    a = jax.nn.sigmoid(p['r_a0'][:, None, None, :] + a_lora)
    kk = (k * p['r_kk']).reshape(b_, t_, C_HEADS, C_HD)
    kk = kk / jnp.maximum(jnp.sqrt(jnp.sum(kk * kk, axis=-1, keepdims=True)), 1e-12)
    k_dir = k[None] * (1.0 + (a - 1.0) * p['r_ka'])
    return dict(k_att=k_att, v_att=v_att, k=k, v=v,
                lw=_heads_major(log_decay), a=_heads_major(a), kd=_heads_major(k_dir),
                kk=jnp.swapaxes(kk, 1, 2), v_h=_heads_major(v))


def _out_side(po, p, rope):
    b_, t_ = po.shape[:2]
    dq, r, u, va, za, zb, zc, gl = _split_cols(po, OUT_SIZES)
    q = _rms_norm(dq.reshape(b_, t_, B_HEADS, 2, B_HD), p['d_qnorm'])
    if rope is not None:
        q = _axial_rope(q, rope)
    q = (q * (B_HD ** -0.5 * LOG2E)).reshape(b_, t_, BR_W).astype(jnp.bfloat16)
    r = _token_shift(r, p['r_conv'][0])
    return dict(q=q, r=r, u=jax.nn.gelu(u), va=jax.nn.gelu(va), za=za, zb=zb, zc=zc, gl=gl)


def _mixer_out(so, st, ks, vs, h0, lam, lam_init, p):
    b_, t_ = so['u'].shape[:2]
    n = b_ * t_
    vn = _layer_norm(so['va'], p['a_ln_g'], p['a_ln_b'])
    s = _gmlp_mix(vn.reshape(n, BR_W), p['a_ws'], p['a_bs']).reshape(b_, t_, BR_W)
    ya = so['u'] * s * jax.nn.silu(so['za'])
    att = _diff_attention(so['q'], ks, vs, lam).reshape(b_, t_, B_HEADS, B_VD)
    att = _rms_norm(att, p['d_subln_g']) * (1.0 - lam_init)
    yb = att.reshape(b_, t_, BR_W) * jax.nn.silu(so['zb'])
    yd, h_fin = _rwkv_scan(st['lw'], st['a'], st['kd'], st['kk'], st['v_h'], _heads_major(so['r']), h0)
    y = jnp.swapaxes(yd[0] + yd[1], 1, 2)
    mu = jnp.mean(y, axis=-1, keepdims=True)
    var = jnp.mean(jnp.square(y - mu), axis=-1, keepdims=True)
    y = ((y - mu) * lax.rsqrt(var + GN_EPS)).reshape(b_, t_, BR_W) * p['r_ln_g'] + p['r_ln_b']
    hd = lambda z: z.reshape(b_, t_, C_HEADS, C_HD)
    bonus = jnp.sum(hd(so['r']) * hd(st['k']) * p['r_rk'], axis=-1, keepdims=True) * hd(st['v'])
    yc = (y + bonus.reshape(b_, t_, BR_W)) * jax.nn.silu(so['zc'])
    g = jax.nn.sigmoid(so['gl'].reshape(n, N_BRANCH, D_MODEL))
    mix = sum(g[:, i] * _matmul(yi.reshape(n, BR_W), p['w_br'][i]) for i, yi in enumerate((ya, yb, yc)))
    return _matmul(mix, p['w_out']).reshape(b_, t_, D_MODEL), h_fin


def _layer(x, xc, c_act, cc_act, rope, lam_init, p, update_ctx):
    d = D_MODEL
    b_, t_ = x.shape[:2]
    tc = xc.shape[1]
    n_pad = -b_ % 8
    mod = _matmul(jnp.pad(c_act, ((0, n_pad), (0, 0))), p['w_mod'])[:b_] + p['b_mod']
    shift, scale, gate = jnp.split(mod[:, None, :], 3, axis=-1)
    mod_c = _matmul(jnp.broadcast_to(cc_act[None], (8, d)), p['w_mod'])[0] + p['b_mod']
    h = _rms_norm(x, p['norm_g']) * (1.0 + scale) + shift
    hc = _rms_norm(xc, p['norm_g']) * (1.0 + mod_c[d:2 * d]) + mod_c[:d]
    lp = p['d_lam']
    lam = jnp.exp(jnp.sum(lp[0] * lp[1])) - jnp.exp(jnp.sum(lp[2] * lp[3])) + lam_init
    h_zero = jnp.zeros((N_DIR, b_, C_HEADS, C_HD, C_HD), jnp.float32)

    pc = _matmul(hc.reshape(b_ * tc, d), p['w_in']).reshape(b_, tc, -1)
    st_c = _state_side(pc[..., :STATE_COLS], p, None)
    so_c = _out_side(pc[..., STATE_COLS:], p, None)
    out_c, h_ctx = _mixer_out(so_c, st_c, [st_c['k_att']], [st_c['v_att']], h_zero, lam, lam_init, p)
    xc_next = xc + mod_c[2 * d:] * out_c if update_ctx else None

    pl_ = _matmul(h.reshape(b_ * t_, d), p['w_in']).reshape(b_, t_, -1)
    st = _state_side(pl_[..., :STATE_COLS], p, rope)
    so = _out_side(pl_[..., STATE_COLS:], p, rope)
    out, _ = _mixer_out(so, st, [st['k_att'], st_c['k_att']], [st['v_att'], st_c['v_att']],
                        h_ctx, lam, lam_init, p)
    return x + gate * out, xc_next


def kernel(x, c, ctx, c_ctx, w_mod, b_mod, norm_g, w_in, a_ln_g, a_ln_b, a_ws, a_bs, d_qnorm, d_knorm, d_lam,
           d_subln_g, r_conv, r_w0, r_w2, r_a0, r_a2, r_kk, r_ka, r_rk, r_ln_g, r_ln_b, w_br, w_out):
    n_tok = x.shape[1]
    rows = n_tok // GRID_W
    row = jnp.broadcast_to(jnp.arange(rows)[:, None], (rows, GRID_W)).reshape(-1)
    col = jnp.broadcast_to(jnp.arange(GRID_W)[None, :], (rows, GRID_W)).reshape(-1)
    rope = (_rope_table(row), _rope_table(col))
    c_act = jax.nn.silu(c)
    cc_act = jax.nn.silu(c_ctx)
    xc = ctx
    depth = w_in.shape[0]
    for l in range(depth):
        p = dict(w_mod=w_mod[l], b_mod=b_mod[l], norm_g=norm_g[l], w_in=w_in[l],
                 a_ln_g=a_ln_g[l], a_ln_b=a_ln_b[l], a_ws=a_ws[l], a_bs=a_bs[l],
                 d_qnorm=d_qnorm[l], d_knorm=d_knorm[l], d_lam=d_lam[l], d_subln_g=d_subln_g[l],
                 r_conv=r_conv[l], r_w0=r_w0[l], r_w2=r_w2[l], r_a0=r_a0[l], r_a2=r_a2[l],
                 r_kk=r_kk[l], r_ka=r_ka[l], r_rk=r_rk[l], r_ln_g=r_ln_g[l], r_ln_b=r_ln_b[l],
                 w_br=w_br[l], w_out=w_out[l])
        lam_init = 0.8 - 0.6 * math.exp(-0.3 * l)
        x, xc = _layer(x, xc, c_act, cc_act, rope, lam_init, p, l < depth - 1)
    return x
```

```python
import functools
import math

import jax
import jax.numpy as jnp
from jax import lax
from jax.experimental import pallas as pl
from jax.experimental.pallas import tpu as pltpu

D_MODEL = 1024
GRID_W = 64
N_BRANCH = 3
BR_W = D_MODEL // 2
A_GROUPS = 4
A_GW = BR_W // A_GROUPS
A_CHUNK = 128
B_HD = 64
B_VD = 2 * B_HD
B_HEADS = BR_W // B_VD
ROPE_BASE = 10000.0
C_HD = 64
C_HEADS = BR_W // C_HD
N_DIR = 2
W_LORA = 64
A_LORA = 64
NORM_EPS = 1e-6
LN_EPS = 1e-5
GN_EPS = 64e-5
STATE_SIZES = (BR_W, BR_W, BR_W, BR_W, N_DIR * W_LORA, N_DIR * A_LORA)
OUT_SIZES = (BR_W,) * 7 + (N_BRANCH * D_MODEL,)
STATE_COLS = sum(STATE_SIZES)
OUT_COLS = sum(OUT_SIZES)

SCAN_CHUNK = 64
ATTN_TQ = 512
ATTN_ROWS = 256
VMEM_LIMIT_BYTES = 48 * 1024 * 1024
LOG2E = 1.4426950408889634


def _tile(n, pref):
    if n <= pref:
        return n
    t = pref
    while n % t:
        t //= 2
    return t


def _mm_kernel(a_ref, b_ref, o_ref):
    o_ref[...] = jnp.dot(a_ref[...], b_ref[...], preferred_element_type=jnp.float32)


def _matmul(a, b, tm=512, tn=1280):
    a = a.astype(jnp.bfloat16)
    b = b.astype(jnp.bfloat16)
    m, k = a.shape
    n = b.shape[1]
    tm = _tile(m, tm)
    tn = n if n <= tn else (tn if n % tn == 0 else _tile(n, 1024))
    return pl.pallas_call(
        _mm_kernel,
        grid=(m // tm, n // tn),
        in_specs=[pl.BlockSpec((tm, k), lambda i, j: (i, 0)),
                  pl.BlockSpec((k, tn), lambda i, j: (0, j))],
        out_specs=pl.BlockSpec((tm, tn), lambda i, j: (i, j)),
        out_shape=jax.ShapeDtypeStruct((m, n), jnp.float32),
        compiler_params=pltpu.CompilerParams(
            dimension_semantics=("parallel", "parallel"), vmem_limit_bytes=VMEM_LIMIT_BYTES),
        name="matmul",
    )(a, b)


def _attn_kernel(lam_ref, q_ref, *refs, chunks):
    n_src = len(chunks)
    k_refs = refs[:n_src]
    v_refs = refs[n_src:2 * n_src]
    o_ref = refs[2 * n_src]
    q = q_ref[0]
    tq = q.shape[0]
    rows = min(tq, ATTN_ROWS)
    lane = lax.broadcasted_iota(jnp.int32, (1, B_VD), 1)
    zero = jnp.zeros_like(q)
    q1 = jnp.where(lane < B_HD, q, zero)
    q2 = jnp.where(lane >= B_HD, q, zero)
    qs = [qj[r:r + rows] for r in range(0, tq, rows) for qj in (q1, q2)]

    def update(carry, k, v):
        ss = [lax.dot_general(qj, k, (((1,), (1,)), ((), ())), preferred_element_type=jnp.float32) for qj in qs]
        m_new = [jnp.maximum(m, jnp.max(s, axis=-1, keepdims=True)) for s, (m, _, _) in zip(ss, carry)]
        ps = [jnp.exp2(s - mn) for s, mn in zip(ss, m_new)]
        pv = [jnp.dot(p.astype(v.dtype), v, preferred_element_type=jnp.float32) for p in ps]
        new = []
        for j, (m, l, acc) in enumerate(carry):
            alpha = jnp.exp2(m - m_new[j])
            new.append((m_new[j], alpha * l + jnp.sum(ps[j], axis=-1, keepdims=True), alpha * acc + pv[j]))
        return tuple(new)

    carry = tuple((jnp.full((rows, 1), -jnp.inf, jnp.float32), jnp.zeros((rows, 1), jnp.float32),
                   jnp.zeros((rows, B_VD), jnp.float32)) for _ in qs)
    for k_ref, v_ref, (n_chunk, tk) in zip(k_refs, v_refs, chunks):
        def body(i, c, k_ref=k_ref, v_ref=v_ref, tk=tk):
            off = pl.multiple_of(i * tk, tk)
            return update(c, k_ref[0, pl.ds(off, tk), :], v_ref[0, pl.ds(off, tk), :])
        carry = lax.fori_loop(0, n_chunk, body, carry)
    lam = lam_ref[0]
    for b in range(tq // rows):
        (_, l1, acc1), (_, l2, acc2) = carry[2 * b], carry[2 * b + 1]
        o_ref[0, b * rows:(b + 1) * rows, :] = acc1 / l1 - lam * (acc2 / l2)


def _diff_attention(q, ks, vs, lam):
    b_, t_, _ = q.shape
    tq = _tile(t_, ATTN_TQ)
    chunks = tuple((k.shape[1] // _tile(k.shape[1], 512), _tile(k.shape[1], 512)) for k in ks)
    kv_specs = [pl.BlockSpec((1, k.shape[1], B_VD), lambda b, h, i: (b, 0, h)) for k in ks]
    return pl.pallas_call(
        functools.partial(_attn_kernel, chunks=chunks),
        grid=(b_, B_HEADS, t_ // tq),
        in_specs=[pl.BlockSpec(memory_space=pltpu.SMEM),
                  pl.BlockSpec((1, tq, B_VD), lambda b, h, i: (b, i, h))] + kv_specs + kv_specs,
        out_specs=pl.BlockSpec((1, tq, B_VD), lambda b, h, i: (b, i, h)),
        out_shape=jax.ShapeDtypeStruct((b_, t_, BR_W), jnp.float32),
        compiler_params=pltpu.CompilerParams(
            dimension_semantics=("parallel", "parallel", "parallel"), vmem_limit_bytes=VMEM_LIMIT_BYTES),
        name="diff_attention",
    )(lam.reshape(1).astype(jnp.float32), q, *ks, *vs)


def _gmlp_kernel(v_ref, ws_ref, bias_ref, o_ref):
    rows = v_ref.shape[0]
    for n in range(rows // A_CHUNK):
        r0 = n * A_CHUNK
        for g in range(A_GROUPS):
            c0 = g * A_GW
            s = jnp.dot(ws_ref[g], v_ref[r0:r0 + A_CHUNK, c0:c0 + A_GW], preferred_element_type=jnp.float32)
            o_ref[r0:r0 + A_CHUNK, c0:c0 + A_GW] = s + bias_ref[:, c0:c0 + A_GW]


def _gmlp_mix(vn, ws, bs):
    n = vn.shape[0]
    rows = _tile(n, 512)
    bias = jnp.repeat(jnp.swapaxes(bs, 0, 1), A_GW, axis=1).astype(jnp.float32)
    return pl.pallas_call(
        _gmlp_kernel,
        grid=(n // rows,),
        in_specs=[pl.BlockSpec((rows, BR_W), lambda i: (i, 0)),
                  pl.BlockSpec((A_GROUPS, A_CHUNK, A_CHUNK), lambda i: (0, 0, 0)),
                  pl.BlockSpec((A_CHUNK, BR_W), lambda i: (0, 0))],
        out_specs=pl.BlockSpec((rows, BR_W), lambda i: (i, 0)),
        out_shape=jax.ShapeDtypeStruct((n, BR_W), jnp.float32),
        compiler_params=pltpu.CompilerParams(
            dimension_semantics=("parallel",), vmem_limit_bytes=VMEM_LIMIT_BYTES),
        name="gmlp_mix",
    )(vn.astype(jnp.bfloat16), ws.astype(jnp.bfloat16), bias)


N_PAIR = C_HEADS // 2
PAIR_W = 2 * C_HD


def _bf(x):
    return x.astype(jnp.bfloat16)


def _dot(a, b):
    return jnp.dot(_bf(a), _bf(b), preferred_element_type=jnp.float32)


def _dot_nt(a, b):
    return lax.dot_general(_bf(a), _bf(b), (((1,), (1,)), ((), ())), preferred_element_type=jnp.float32)


def _split(x):
    hi = _bf(x)
    return hi, _bf(x - hi.astype(jnp.float32))


def _dot3(a, b):
    ah, al = _split(a)
    bh, bl = _split(b)
    f = lambda x, y: jnp.dot(x, y, preferred_element_type=jnp.float32)
    return f(ah, bh) + (f(al, bh) + f(ah, bl))


def _scan_kernel(lw0_ref, lw1_ref, a0_ref, a1_ref, kd0_ref, kd1_ref, kk0_ref, kk1_ref, v0_ref, v1_ref,
                 r0_ref, r1_ref, s0_ref, y0_ref, y1_ref, s_ref):
    @pl.when(pl.program_id(1) == 0)
    def _():
        s_ref[...] = s0_ref[...]

    ck = lw0_ref.shape[0]
    n2 = 2 * ck
    row = lax.broadcasted_iota(jnp.int32, (n2, n2), 0)
    col = lax.broadcasted_iota(jnp.int32, (n2, n2), 1)
    gap = row % ck - col % ck
    strict = (gap > 0, gap < 0)
    incl = (gap >= 0, gap <= 0)
    eye = (row == col).astype(jnp.float32)
    gap1 = lax.broadcasted_iota(jnp.int32, (ck, ck), 0) - lax.broadcasted_iota(jnp.int32, (ck, ck), 1)
    incl1 = (_bf((gap1 >= 0).astype(jnp.float32)), _bf((gap1 <= 0).astype(jnp.float32)))
    head0 = lax.broadcasted_iota(jnp.int32, (1, PAIR_W), 1) < C_HD

    def stack(x):
        return jnp.concatenate([jnp.where(head0, x, 0.0), jnp.where(head0, 0.0, x)], axis=0)

    def scaled(d, lw_ref, a_ref, kd_ref, kk_ref, v_ref, r_ref):
        lw = lw_ref[...]
        lw_hi, lw_lo = _split(lw)
        cum = (jnp.dot(incl1[d], lw_hi, preferred_element_type=jnp.float32)
               + jnp.dot(incl1[d], lw_lo, preferred_element_type=jnp.float32))
        p_in = jnp.exp(cum)
        p_inv = jnp.exp(-cum)
        kk = kk_ref[...]
        return dict(at=-kk * jnp.exp(cum - lw), bt=kk * a_ref[...] * p_inv, kt=kd_ref[...] * p_inv,
                    rt=r_ref[...] * p_in, v=v_ref[...], p_tot=jnp.exp(jnp.sum(lw, axis=0, keepdims=True)))

    q = (scaled(0, lw0_ref, a0_ref, kd0_ref, kk0_ref, v0_ref, r0_ref),
         scaled(1, lw1_ref, a1_ref, kd1_ref, kk1_ref, v1_ref, r1_ref))
    y_refs = (y0_ref, y1_ref)

    chains = [(d, p) for p in range(N_PAIR) for d in range(N_DIR)]
    ids = range(len(chains))
    sl = [slice(p * PAIR_W, (p + 1) * PAIR_W) for _, p in chains]
    atx, btx, ktx, rtx, vx = ([stack(q[d][name][:, sl[i]]) for i, (d, _) in enumerate(chains)]
                              for name in ('at', 'bt', 'kt', 'rt', 'v'))
    big = [_dot_nt(jnp.concatenate([atx[i], rtx[i]], axis=0), jnp.concatenate([btx[i], ktx[i]], axis=0))
           for i in ids]
    a_ab = [jnp.where(strict[chains[i][0]], big[i][:n2, :n2], 0.0) for i in ids]
    a_ak = [jnp.where(strict[chains[i][0]], big[i][:n2, n2:], 0.0) for i in ids]
    a_rb = [jnp.where(incl[chains[i][0]], big[i][n2:, :n2], 0.0) for i in ids]
    a_rk = [jnp.where(incl[chains[i][0]], big[i][n2:, n2:], 0.0) for i in ids]

    tinv = [eye + a_ab[i] for i in ids]
    apow = a_ab
    span = 2
    while span < ck:
        apow = [_dot(apow[i], apow[i]) for i in ids]
        tinv = [tinv[i] + _dot(apow[i], tinv[i]) for i in ids]
        span *= 2
    resid = [(eye - tinv[i]) + _dot3(a_ab[i], tinv[i]) for i in ids]
    tinv = [tinv[i] + _dot(tinv[i], resid[i]) for i in ids]

    av = [_dot(jnp.concatenate([a_ak[i], a_rk[i]], axis=0), vx[i]) for i in ids]
    wu = [_dot(tinv[i], jnp.concatenate([atx[i], av[i][:n2]], axis=1)) for i in ids]
    kv = [_dot(jnp.transpose(vx[i]), ktx[i]) for i in ids]

    s = [s_ref[d, p] for d, p in chains]
    g = [_dot_nt(jnp.concatenate([wu[i][:, :PAIR_W], rtx[i]], axis=0), s[i]) for i in ids]
    u = [g[i][:n2] + wu[i][:, PAIR_W:] for i in ids]
    su = [_dot(jnp.transpose(u[i]), btx[i]) for i in ids]
    yx = [g[i][n2:] + _dot(a_rb[i], u[i]) + av[i][n2:] for i in ids]
    for i, (d, p) in enumerate(chains):
        s_ref[d, p] = (s[i] + su[i] + kv[i]) * q[d]['p_tot'][:, sl[i]]
        y_refs[d][:, sl[i]] = yx[i][:ck] + yx[i][ck:]


def _rwkv_scan(lw, a, kd, kk, v, r, s0):
    _, b_, t_, w_ = lw.shape
    ck = _tile(t_, SCAN_CHUNK)
    nc = t_ // ck
    fwd = pl.BlockSpec((None, None, ck, w_), lambda b, c: (0, b, c, 0))
    bwd = pl.BlockSpec((None, None, ck, w_), lambda b, c: (1, b, nc - 1 - c, 0))
    fwd_shared = pl.BlockSpec((None, ck, w_), lambda b, c: (b, c, 0))
    bwd_shared = pl.BlockSpec((None, ck, w_), lambda b, c: (b, nc - 1 - c, 0))
    state_spec = pl.BlockSpec((N_DIR, None, N_PAIR, PAIR_W, PAIR_W), lambda b, c: (0, b, 0, 0, 0))
    y0, y1, s_fin = pl.pallas_call(
        _scan_kernel,
        grid=(b_, nc),
        in_specs=[fwd, bwd, fwd, bwd, fwd, bwd, fwd_shared, bwd_shared, fwd_shared, bwd_shared,
                  fwd_shared, bwd_shared, state_spec],
        out_specs=[fwd_shared, bwd_shared, state_spec],
        out_shape=[jax.ShapeDtypeStruct((b_, t_, w_), jnp.float32), jax.ShapeDtypeStruct((b_, t_, w_), jnp.float32),
                   jax.ShapeDtypeStruct((N_DIR, b_, N_PAIR, PAIR_W, PAIR_W), jnp.float32)],
        compiler_params=pltpu.CompilerParams(
            dimension_semantics=("parallel", "arbitrary"), vmem_limit_bytes=VMEM_LIMIT_BYTES),
        name="rwkv_scan",
    )(lw, lw, a, a, kd, kd, kk, kk, v, v, r, r, s0)
    return y0, y1, s_fin


def _rms_norm(x, g):
    return x * lax.rsqrt(jnp.mean(x * x, axis=-1, keepdims=True) + NORM_EPS) * g


def _layer_norm(x, g, b):
    mu = jnp.mean(x, axis=-1, keepdims=True)
    var = jnp.mean(jnp.square(x - mu), axis=-1, keepdims=True)
    return (x - mu) * lax.rsqrt(var + LN_EPS) * g + b


def _split_cols(t, sizes):
    out, o = [], 0
    for s in sizes:
        out.append(t[..., o:o + s])
        o += s
    return out


def _rope_table(pos):
    nf = B_HD // 4
    inv = ROPE_BASE ** (-jnp.arange(nf, dtype=jnp.float32) / nf)
    ang = pos.astype(jnp.float32)[:, None] * inv[None, :]
    return jnp.cos(ang)[:, None, None, :], jnp.sin(ang)[:, None, None, :]


def _rotate(x, cos, sin):
    x1, x2 = jnp.split(x, 2, axis=-1)
    return jnp.concatenate([x1 * cos - x2 * sin, x1 * sin + x2 * cos], axis=-1)


def _axial_rope(x, rope):
    (rc, rs), (cc, cs) = rope
    x_row, x_col = jnp.split(x, 2, axis=-1)
    return jnp.concatenate([_rotate(x_row, rc, rs), _rotate(x_col, cc, cs)], axis=-1)


def _token_shift(x, taps):
    xp = jnp.pad(x, ((0, 0), (1, 1), (0, 0)))
    return xp[:, :-2] * taps[0] + xp[:, 1:-1] * taps[1] + xp[:, 2:] * taps[2]


def _state_side(ps, p, rope):
    b_, t_ = ps.shape[:2]
    dk, dv, kr, vr, wl, al = _split_cols(ps, STATE_SIZES)
    k_att = _rms_norm(dk.reshape(b_, t_, B_HEADS, 2, B_HD), p['d_knorm'])
    if rope is not None:
        k_att = _axial_rope(k_att, rope)
    k_att = k_att.reshape(b_, t_, BR_W).astype(jnp.bfloat16)
    v_att = dv.astype(jnp.bfloat16)
    k = _token_shift(kr, p['r_conv'][1])
    v = _token_shift(vr, p['r_conv'][2])
    wl = jnp.tanh(wl).reshape(b_ * t_, N_DIR, W_LORA)
    al = al.reshape(b_ * t_, N_DIR, A_LORA)
    w_lora = jnp.stack([_matmul(wl[:, d], p['r_w2'][d]) for d in range(N_DIR)]).reshape(N_DIR, b_, t_, BR_W)
    a_lora = jnp.stack([_matmul(al[:, d], p['r_a2'][d]) for d in range(N_DIR)]).reshape(N_DIR, b_, t_, BR_W)
    w_log = p['r_w0'][:, None, None, :] + w_lora
    log_decay = -jnp.exp(-jax.nn.softplus(-w_log) - 0.5)
    a = jax.nn.sigmoid(p['r_a0'][:, None, None, :] + a_lora)
    kk = (k * p['r_kk']).reshape(b_, t_, C_HEADS, C_HD)
    kk = kk / jnp.maximum(jnp.sqrt(jnp.sum(kk * kk, axis=-1, keepdims=True)), 1e-12)
    k_dir = k[None] * (1.0 + (a - 1.0) * p['r_ka'])
    return dict(k_att=k_att, v_att=v_att, k=k, v=v, lw=log_decay, a=a, kd=k_dir, kk=kk.reshape(b_, t_, BR_W))


def _out_side(po, p, rope):
    b_, t_ = po.shape[:2]
    dq, r, u, va, za, zb, zc, gl = _split_cols(po, OUT_SIZES)
    q = _rms_norm(dq.reshape(b_, t_, B_HEADS, 2, B_HD), p['d_qnorm'])
    if rope is not None:
        q = _axial_rope(q, rope)
    q = (q * (B_HD ** -0.5 * LOG2E)).reshape(b_, t_, BR_W).astype(jnp.bfloat16)
    r = _token_shift(r, p['r_conv'][0])
    return dict(q=q, r=r, u=jax.nn.gelu(u), va=jax.nn.gelu(va), za=za, zb=zb, zc=zc, gl=gl)


def _mixer_out(so, st, ks, vs, h0, lam, lam_init, p):
    b_, t_ = so['u'].shape[:2]
    n = b_ * t_
    vn = _layer_norm(so['va'], p['a_ln_g'], p['a_ln_b'])
    s = _gmlp_mix(vn.reshape(n, BR_W), p['a_ws'], p['a_bs']).reshape(b_, t_, BR_W)
    ya = so['u'] * s * jax.nn.silu(so['za'])
    att = _diff_attention(so['q'], ks, vs, lam).reshape(b_, t_, B_HEADS, B_VD)
    att = _rms_norm(att, p['d_subln_g']) * (1.0 - lam_init)
    yb = att.reshape(b_, t_, BR_W) * jax.nn.silu(so['zb'])
    y0, y1, h_fin = _rwkv_scan(st['lw'], st['a'], st['kd'], st['kk'], st['v'], so['r'], h0)
    y = (y0 + y1).reshape(b_, t_, C_HEADS, C_HD)
    mu = jnp.mean(y, axis=-1, keepdims=True)
    var = jnp.mean(jnp.square(y - mu), axis=-1, keepdims=True)
    y = ((y - mu) * lax.rsqrt(var + GN_EPS)).reshape(b_, t_, BR_W) * p['r_ln_g'] + p['r_ln_b']
    hd = lambda z: z.reshape(b_, t_, C_HEADS, C_HD)
    bonus = jnp.sum(hd(so['r']) * hd(st['k']) * p['r_rk'], axis=-1, keepdims=True) * hd(st['v'])
    yc = (y + bonus.reshape(b_, t_, BR_W)) * jax.nn.silu(so['zc'])
    g = jax.nn.sigmoid(so['gl'].reshape(n, N_BRANCH, D_MODEL))
    mix = sum(g[:, i] * _matmul(yi.reshape(n, BR_W), p['w_br'][i]) for i, yi in enumerate((ya, yb, yc)))
    return _matmul(mix, p['w_out']).reshape(b_, t_, D_MODEL), h_fin


def _layer(x, xc, c_act, cc_act, rope, lam_init, p, update_ctx):
    d = D_MODEL
    b_, t_ = x.shape[:2]
    tc = xc.shape[1]
    n_pad = -b_ % 8
    mod = _matmul(jnp.pad(c_act, ((0, n_pad), (0, 0))), p['w_mod'])[:b_] + p['b_mod']
    shift, scale, gate = jnp.split(mod[:, None, :], 3, axis=-1)
    mod_c = _matmul(jnp.broadcast_to(cc_act[None], (8, d)), p['w_mod'])[0] + p['b_mod']
    h = _rms_norm(x, p['norm_g']) * (1.0 + scale) + shift
    hc = _rms_norm(xc, p['norm_g']) * (1.0 + mod_c[d:2 * d]) + mod_c[:d]
    lp = p['d_lam']
    lam = jnp.exp(jnp.sum(lp[0] * lp[1])) - jnp.exp(jnp.sum(lp[2] * lp[3])) + lam_init
    h_zero = jnp.zeros((N_DIR, b_, N_PAIR, PAIR_W, PAIR_W), jnp.float32)

    pc = _matmul(hc.reshape(b_ * tc, d), p['w_in']).reshape(b_, tc, -1)
    st_c = _state_side(pc[..., :STATE_COLS], p, None)
    so_c = _out_side(pc[..., STATE_COLS:], p, None)
    out_c, h_ctx = _mixer_out(so_c, st_c, [st_c['k_att']], [st_c['v_att']], h_zero, lam, lam_init, p)
    xc_next = xc + mod_c[2 * d:] * out_c if update_ctx else None

    pl_ = _matmul(h.reshape(b_ * t_, d), p['w_in']).reshape(b_, t_, -1)
    st = _state_side(pl_[..., :STATE_COLS], p, rope)
    so = _out_side(pl_[..., STATE_COLS:], p, rope)
    out, _ = _mixer_out(so, st, [st['k_att'], st_c['k_att']], [st['v_att'], st_c['v_att']],
                        h_ctx, lam, lam_init, p)
    return x + gate * out, xc_next


def kernel(x, c, ctx, c_ctx, w_mod, b_mod, norm_g, w_in, a_ln_g, a_ln_b, a_ws, a_bs, d_qnorm, d_knorm, d_lam,
           d_subln_g, r_conv, r_w0, r_w2, r_a0, r_a2, r_kk, r_ka, r_rk, r_ln_g, r_ln_b, w_br, w_out):
    n_tok = x.shape[1]
    rows = n_tok // GRID_W
    row = jnp.broadcast_to(jnp.arange(rows)[:, None], (rows, GRID_W)).reshape(-1)
    col = jnp.broadcast_to(jnp.arange(GRID_W)[None, :], (rows, GRID_W)).reshape(-1)
    rope = (_rope_table(row), _rope_table(col))
    c_act = jax.nn.silu(c)
    cc_act = jax.nn.silu(c_ctx)
    xc = ctx
    depth = w_in.shape[0]
    for l in range(depth):
        p = dict(w_mod=w_mod[l], b_mod=b_mod[l], norm_g=norm_g[l], w_in=w_in[l],
                 a_ln_g=a_ln_g[l], a_ln_b=a_ln_b[l], a_ws=a_ws[l], a_bs=a_bs[l],
                 d_qnorm=d_qnorm[l], d_knorm=d_knorm[l], d_lam=d_lam[l], d_subln_g=d_subln_g[l],
                 r_conv=r_conv[l], r_w0=r_w0[l], r_w2=r_w2[l], r_a0=r_a0[l], r_a2=r_a2[l],
                 r_kk=r_kk[l], r_ka=r_ka[l], r_rk=r_rk[l], r_ln_g=r_ln_g[l], r_ln_b=r_ln_b[l],
                 w_br=w_br[l], w_out=w_out[l])
        lam_init = 0.8 - 0.6 * math.exp(-0.3 * l)
        x, xc = _layer(x, xc, c_act, cc_act, rope, lam_init, p, l < depth - 1)
    return x
```

```python
import functools
import math

import jax
import jax.numpy as jnp
from jax import lax
from jax.experimental import pallas as pl
from jax.experimental.pallas import tpu as pltpu

D_MODEL = 1024
GRID_W = 64
N_BRANCH = 3
BR_W = D_MODEL // 2
A_GROUPS = 4
A_GW = BR_W // A_GROUPS
A_CHUNK = 128
B_HD = 64
B_VD = 2 * B_HD
B_HEADS = BR_W // B_VD
ROPE_BASE = 10000.0
C_HD = 64
C_HEADS = BR_W // C_HD
N_DIR = 2
W_LORA = 64
A_LORA = 64
NORM_EPS = 1e-6
LN_EPS = 1e-5
GN_EPS = 64e-5
STATE_SIZES = (BR_W, BR_W, BR_W, BR_W, N_DIR * W_LORA, N_DIR * A_LORA)
OUT_SIZES = (BR_W,) * 7 + (N_BRANCH * D_MODEL,)
STATE_COLS = sum(STATE_SIZES)
OUT_COLS = sum(OUT_SIZES)

ROW_TILE = 256
SCAN_CHUNK = 64
ATTN_TQ = 512
ATTN_ROWS = 256
VMEM_LIMIT_BYTES = 48 * 1024 * 1024
LOG2E = 1.4426950408889634
DECAY_SCALE = math.exp(-0.5)


def _tile(n, pref):
    if n <= pref:
        return n
    t = pref
    while n % t:
        t //= 2
    return t


def _bf(x):
    return x.astype(jnp.bfloat16)


def _dot(a, b):
    return jnp.dot(_bf(a), _bf(b), preferred_element_type=jnp.float32)


def _dot_nt(a, b):
    return lax.dot_general(_bf(a), _bf(b), (((1,), (1,)), ((), ())), preferred_element_type=jnp.float32)


def _split(x):
    hi = _bf(x)
    return hi, _bf(x - hi.astype(jnp.float32))


def _dot3(a, b):
    ah, al = _split(a)
    bh, bl = _split(b)
    f = lambda x, y: jnp.dot(x, y, preferred_element_type=jnp.float32)
    return f(ah, bh) + (f(al, bh) + f(ah, bl))


def _group_sum(x, ones_ref):
    hi, lo = _split(x)
    g = ones_ref[...]
    return (jnp.dot(hi, g, preferred_element_type=jnp.float32) + jnp.dot(lo, g, preferred_element_type=jnp.float32))


def _params(*semantics):
    return pltpu.CompilerParams(dimension_semantics=semantics, vmem_limit_bytes=VMEM_LIMIT_BYTES)


def _mod_kernel(a_ref, w_ref, b_ref, o_ref):
    o_ref[...] = jnp.dot(a_ref[...], w_ref[...], preferred_element_type=jnp.float32) + b_ref[...]


def _mod_rows(act, w_mod, b_mod):
    r = act.shape[0]
    a = _bf(jnp.pad(act, ((0, -r % 16), (0, 0))))
    w = _bf(w_mod)
    b = b_mod.reshape(1, -1)
    out = pl.pallas_call(
        _mod_kernel,
        grid=(1,),
        in_specs=[_const_spec(a), _const_spec(w), _const_spec(b)],
        out_specs=pl.BlockSpec((a.shape[0], w.shape[1]), lambda i: (0, 0)),
        out_shape=jax.ShapeDtypeStruct((a.shape[0], w.shape[1]), jnp.float32),
        compiler_params=_params("arbitrary"), name="mod_rows",
    )(a, w, b)
    return out[:r]


def _modulated(x_ref, gs_ref, sh_ref):
    x = x_ref[...]
    rinv = lax.rsqrt(jnp.mean(x * x, axis=-1, keepdims=True) + NORM_EPS)
    return _bf(x * rinv * gs_ref[...] + sh_ref[...])


def _qkv_kernel(x_ref, gs_ref, sh_ref, wk_ref, wv_ref, wq_ref, ones_ref, gk_ref, gq_ref, *rest, rope):
    if rope:
        cos_ref, sin_ref, k_out, v_out, q_out = rest
    else:
        k_out, v_out, q_out = rest
    h = _modulated(x_ref, gs_ref, sh_ref)

    if rope:
        cos = jnp.concatenate([cos_ref[...]] * B_HEADS, axis=1)
        sin = jnp.concatenate([sin_ref[...]] * B_HEADS, axis=1)
        lane = lax.broadcasted_iota(jnp.int32, (1, BR_W), 1)
        first = (lane % (B_HD // 2)) < (B_HD // 4)

    def normed(z, g_ref):
        ss = _group_sum(z * z, ones_ref)
        y = z * lax.rsqrt(ss * (1.0 / B_HD) + NORM_EPS) * g_ref[...]
        if rope:
            partner = jnp.where(first, pltpu.roll(y, BR_W - B_HD // 4, 1), pltpu.roll(y, B_HD // 4, 1))
            y = y * cos + partner * sin
        return y

    k_out[...] = _bf(normed(jnp.dot(h, wk_ref[...], preferred_element_type=jnp.float32), gk_ref))
    v_out[...] = _bf(jnp.dot(h, wv_ref[...], preferred_element_type=jnp.float32))
    q_out[...] = _bf(normed(jnp.dot(h, wq_ref[...], preferred_element_type=jnp.float32), gq_ref)
                     * (B_HD ** -0.5 * LOG2E))


def _rwkv_in_kernel(x_ref, gs_ref, sh_ref, wk_ref, wv_ref, wr_ref, wla_ref, w2_ref, a2_ref, w0_ref, a0_ref,
                    k_out, v_out, r_out, lw_out, a_out):
    h = _modulated(x_ref, gs_ref, sh_ref)
    k_out[...] = jnp.dot(h, wk_ref[...], preferred_element_type=jnp.float32)
    v_out[...] = jnp.dot(h, wv_ref[...], preferred_element_type=jnp.float32)
    r_out[...] = jnp.dot(h, wr_ref[...], preferred_element_type=jnp.float32)
    la = jnp.dot(h, wla_ref[...], preferred_element_type=jnp.float32)
    n_l = N_DIR * W_LORA
    wl = _bf(jnp.tanh(la[:, :n_l]))
    al = _bf(la[:, n_l:])
    for d in range(N_DIR):
        w_log = w0_ref[d] + jnp.dot(wl, w2_ref[d], preferred_element_type=jnp.float32)
        lw_out[d] = -DECAY_SCALE * jax.nn.sigmoid(w_log)
        a_out[d] = jax.nn.sigmoid(a0_ref[d] + jnp.dot(al, a2_ref[d], preferred_element_type=jnp.float32))


def _gates_kernel(x_ref, gs_ref, sh_ref, w_ref, lng_ref, lnb_ref, uz_out, vn_out, szb_out, szc_out):
    h = _modulated(x_ref, gs_ref, sh_ref)
    z = jnp.dot(h, w_ref[...], preferred_element_type=jnp.float32)
    u, va, za, zb, zc = (z[:, i * BR_W:(i + 1) * BR_W] for i in range(5))
    uz_out[...] = _bf(jax.nn.gelu(u) * jax.nn.silu(za))
    va = jax.nn.gelu(va)
    mu = jnp.mean(va, axis=-1, keepdims=True)
    var = jnp.mean(jnp.square(va - mu), axis=-1, keepdims=True)
    vn_out[...] = _bf((va - mu) * lax.rsqrt(var + LN_EPS) * lng_ref[...] + lnb_ref[...])
    szb_out[...] = _bf(jax.nn.silu(zb))
    szc_out[...] = _bf(jax.nn.silu(zc))


def _row_specs(n, t_, n_mod):
    tm = _tile(t_, ROW_TILE)
    per_batch = t_ // tm
    x_spec = pl.BlockSpec((tm, D_MODEL), lambda i: (i, 0))
    mod_spec = pl.BlockSpec((None, 1, D_MODEL), (lambda i: (i // per_batch, 0, 0)) if n_mod > 1 else (lambda i: (0, 0, 0)))
    return tm, per_batch, x_spec, mod_spec


def _const_spec(a):
    nd = a.ndim
    return pl.BlockSpec(a.shape, lambda i: (0,) * nd)


def _out_spec(tm, w):
    return pl.BlockSpec((tm, w), lambda i: (i, 0))


def _qkv_proj(x2, t_, gs, sh, lw, rope):
    n = x2.shape[0]
    tm, per_batch, x_spec, mod_spec = _row_specs(n, t_, gs.shape[0])
    consts = [lw['wk'], lw['wv'], lw['wq'], lw['ones64'], lw['gk'], lw['gq']]
    in_specs = [x_spec, mod_spec, mod_spec] + [_const_spec(a) for a in consts]
    args = [x2, gs, sh] + consts
    if rope is not None:
        rope_spec = pl.BlockSpec((tm, B_VD), lambda i: (i % per_batch, 0))
        in_specs += [rope_spec, rope_spec]
        args += list(rope)
    out = jax.ShapeDtypeStruct((n, BR_W), jnp.bfloat16)
    return pl.pallas_call(
        functools.partial(_qkv_kernel, rope=rope is not None),
        grid=(n // tm,), in_specs=in_specs, out_specs=[_out_spec(tm, BR_W)] * 3, out_shape=[out] * 3,
        compiler_params=_params("parallel"), name="qkv_proj",
    )(*args)


def _rwkv_in_proj(x2, t_, gs, sh, lw):
    n = x2.shape[0]
    tm, _, x_spec, mod_spec = _row_specs(n, t_, gs.shape[0])
    consts = [lw['wkr'], lw['wvr'], lw['wr'], lw['wla'], lw['w2'], lw['a2'], lw['w0'], lw['a0']]
    f32 = jax.ShapeDtypeStruct((n, BR_W), jnp.float32)
    f32d = jax.ShapeDtypeStruct((N_DIR, n, BR_W), jnp.float32)
    dir_spec = pl.BlockSpec((N_DIR, tm, BR_W), lambda i: (0, i, 0))
    return pl.pallas_call(
        _rwkv_in_kernel,
        grid=(n // tm,), in_specs=[x_spec, mod_spec, mod_spec] + [_const_spec(a) for a in consts],
        out_specs=[_out_spec(tm, BR_W)] * 3 + [dir_spec] * 2, out_shape=[f32] * 3 + [f32d] * 2,
        compiler_params=_params("parallel"), name="rwkv_in_proj",
    )(x2, gs, sh, *consts)


def _gates_proj(x2, t_, gs, sh, lw):
    n = x2.shape[0]
    tm, _, x_spec, mod_spec = _row_specs(n, t_, gs.shape[0])
    consts = [lw['wgates'], lw['a_ln_g'], lw['a_ln_b']]
    out = jax.ShapeDtypeStruct((n, BR_W), jnp.bfloat16)
    return pl.pallas_call(
        _gates_kernel,
        grid=(n // tm,), in_specs=[x_spec, mod_spec, mod_spec] + [_const_spec(a) for a in consts],
        out_specs=[_out_spec(tm, BR_W)] * 4, out_shape=[out] * 4,
        compiler_params=_params("parallel"), name="gates_proj",
    )(x2, gs, sh, *consts)


def _merge_kernel(x_ref, gs_ref, sh_ref, gate_ref, ya_ref, yb_ref, y0_ref, y1_ref, bonus_ref, szc_ref,
                  mean_ref, lng_ref, lnb_ref, wgl_ref, wbr_ref, wout_ref, o_ref):
    x = x_ref[...]
    h = _modulated(x_ref, gs_ref, sh_ref)
    g = jax.nn.sigmoid(jnp.dot(h, wgl_ref[...], preferred_element_type=jnp.float32))
    y = y0_ref[...] + y1_ref[...]
    dev = y - _group_sum(y, mean_ref)
    var = _group_sum(dev * dev, mean_ref)
    yc = (dev * lax.rsqrt(var + GN_EPS) * lng_ref[...] + lnb_ref[...] + bonus_ref[...]) * szc_ref[...].astype(jnp.float32)
    ys = (ya_ref[...], yb_ref[...], _bf(yc))
    mix = None
    for i in range(N_BRANCH):
        up = jnp.dot(ys[i], wbr_ref[i], preferred_element_type=jnp.float32)
        term = g[:, i * D_MODEL:(i + 1) * D_MODEL] * up
        mix = term if mix is None else mix + term
    o_ref[...] = x + gate_ref[...] * jnp.dot(_bf(mix), wout_ref[...], preferred_element_type=jnp.float32)


def _merge(x2, t_, gs, sh, gate, ya, yb, y0, y1, bonus, szc, lw):
    n = x2.shape[0]
    tm, _, x_spec, mod_spec = _row_specs(n, t_, gs.shape[0])
    consts = [lw['mean64'], lw['r_ln_g'], lw['r_ln_b'], lw['wgl'], lw['wbr'], lw['wout']]
    br = _out_spec(tm, BR_W)
    return pl.pallas_call(
        _merge_kernel,
        grid=(n // tm,),
        in_specs=[x_spec, mod_spec, mod_spec, mod_spec] + [br] * 6 + [_const_spec(a) for a in consts],
        out_specs=_out_spec(tm, D_MODEL), out_shape=jax.ShapeDtypeStruct((n, D_MODEL), jnp.float32),
        compiler_params=_params("parallel"), name="merge_out",
    )(x2, gs, sh, gate, ya, yb, y0, y1, bonus, szc, *consts)


def _attn_kernel(lam_ref, q_ref, szb_ref, g_ref, *refs, chunks):
    n_src = len(chunks)
    k_refs = refs[:n_src]
    v_refs = refs[n_src:2 * n_src]
    o_ref = refs[2 * n_src]
    q = q_ref[0]
    tq = q.shape[0]
    rows = min(tq, ATTN_ROWS)
    lane = lax.broadcasted_iota(jnp.int32, (1, B_VD), 1)
    zero = jnp.zeros_like(q)
    q1 = jnp.where(lane < B_HD, q, zero)
    q2 = jnp.where(lane >= B_HD, q, zero)
    qs = [qj[r:r + rows] for r in range(0, tq, rows) for qj in (q1, q2)]

    def update(carry, k, v):
        ss = [lax.dot_general(qj, k, (((1,), (1,)), ((), ())), preferred_element_type=jnp.float32) for qj in qs]
        m_new = [jnp.maximum(m, jnp.max(s, axis=-1, keepdims=True)) for s, (m, _, _) in zip(ss, carry)]
        ps = [jnp.exp2(s - mn) for s, mn in zip(ss, m_new)]
        pv = [jnp.dot(p.astype(v.dtype), v, preferred_element_type=jnp.float32) for p in ps]
        new = []
        for j, (m, l, acc) in enumerate(carry):
            alpha = jnp.exp2(m - m_new[j])
            new.append((m_new[j], alpha * l + jnp.sum(ps[j], axis=-1, keepdims=True), alpha * acc + pv[j]))
        return tuple(new)

    carry = tuple((jnp.full((rows, 1), -jnp.inf, jnp.float32), jnp.zeros((rows, 1), jnp.float32),
                   jnp.zeros((rows, B_VD), jnp.float32)) for _ in qs)
    for k_ref, v_ref, (n_chunk, tk) in zip(k_refs, v_refs, chunks):
        def body(i, c, k_ref=k_ref, v_ref=v_ref, tk=tk):
            off = pl.multiple_of(i * tk, tk)
            return update(c, k_ref[0, pl.ds(off, tk), :], v_ref[0, pl.ds(off, tk), :])
        carry = lax.fori_loop(0, n_chunk, body, carry)
    lam = lam_ref[0]
    for b in range(tq // rows):
        (_, l1, acc1), (_, l2, acc2) = carry[2 * b], carry[2 * b + 1]
        att = acc1 / l1 - lam * (acc2 / l2)
        att = att * lax.rsqrt(jnp.mean(att * att, axis=-1, keepdims=True) + NORM_EPS) * g_ref[...]
        o_ref[0, b * rows:(b + 1) * rows, :] = _bf(att * szb_ref[0, b * rows:(b + 1) * rows, :].astype(jnp.float32))


def _diff_attention(q, szb, subln_g, ks, vs, lam):
    b_, t_, _ = q.shape
    tq = _tile(t_, ATTN_TQ)
    chunks = tuple((k.shape[1] // _tile(k.shape[1], 512), _tile(k.shape[1], 512)) for k in ks)
    kv_specs = [pl.BlockSpec((1, k.shape[1], B_VD), lambda b, h, i: (b, 0, h)) for k in ks]
    q_spec = pl.BlockSpec((1, tq, B_VD), lambda b, h, i: (b, i, h))
    return pl.pallas_call(
        functools.partial(_attn_kernel, chunks=chunks),
        grid=(b_, B_HEADS, t_ // tq),
        in_specs=[pl.BlockSpec(memory_space=pltpu.SMEM), q_spec, q_spec,
                  pl.BlockSpec((1, B_VD), lambda b, h, i: (0, 0))] + kv_specs + kv_specs,
        out_specs=q_spec,
        out_shape=jax.ShapeDtypeStruct((b_, t_, BR_W), jnp.bfloat16),
        compiler_params=_params("parallel", "parallel", "parallel"),
        name="diff_attention",
    )(lam.reshape(1).astype(jnp.float32), q, szb, subln_g, *ks, *vs)


def _gmlp_kernel(v_ref, uz_ref, ws_ref, bias_ref, o_ref):
    rows = v_ref.shape[0]
    for n in range(rows // A_CHUNK):
        r0 = n * A_CHUNK
        for g in range(A_GROUPS):
            c0 = g * A_GW
            s = jnp.dot(ws_ref[g], v_ref[r0:r0 + A_CHUNK, c0:c0 + A_GW], preferred_element_type=jnp.float32)
            s = s + bias_ref[:, c0:c0 + A_GW]
            o_ref[r0:r0 + A_CHUNK, c0:c0 + A_GW] = _bf(uz_ref[r0:r0 + A_CHUNK, c0:c0 + A_GW].astype(jnp.float32) * s)


def _gmlp_branch(vn, uz, ws, bias):
    n = vn.shape[0]
    rows = _tile(n, 512)
    row_spec = pl.BlockSpec((rows, BR_W), lambda i: (i, 0))
    return pl.pallas_call(
        _gmlp_kernel,
        grid=(n // rows,),
        in_specs=[row_spec, row_spec, _const_spec(ws), _const_spec(bias)],
        out_specs=row_spec,
        out_shape=jax.ShapeDtypeStruct((n, BR_W), jnp.bfloat16),
        compiler_params=_params("parallel"),
        name="gmlp_mix",
    )(vn, uz, ws, bias)


N_PAIR = C_HEADS // 2
PAIR_W = 2 * C_HD


def _scan_kernel(lw0_ref, lw1_ref, a0_ref, a1_ref, kd0_ref, kd1_ref, kk0_ref, kk1_ref, v0_ref, v1_ref,
                 r0_ref, r1_ref, s0_ref, y0_ref, y1_ref, s_ref):
    @pl.when(pl.program_id(1) == 0)
    def _():
        s_ref[...] = s0_ref[...]

    ck = lw0_ref.shape[0]
    n2 = 2 * ck
    row = lax.broadcasted_iota(jnp.int32, (n2, n2), 0)
    col = lax.broadcasted_iota(jnp.int32, (n2, n2), 1)
    gap = row % ck - col % ck
    strict = (gap > 0, gap < 0)
    incl = (gap >= 0, gap <= 0)
    eye = (row == col).astype(jnp.float32)
    gap1 = lax.broadcasted_iota(jnp.int32, (ck, ck), 0) - lax.broadcasted_iota(jnp.int32, (ck, ck), 1)
    incl1 = (_bf((gap1 >= 0).astype(jnp.float32)), _bf((gap1 <= 0).astype(jnp.float32)))
    head0 = lax.broadcasted_iota(jnp.int32, (1, PAIR_W), 1) < C_HD

    def stack(x):
        return jnp.concatenate([jnp.where(head0, x, 0.0), jnp.where(head0, 0.0, x)], axis=0)

    def scaled(d, lw_ref, a_ref, kd_ref, kk_ref, v_ref, r_ref):
        lw = lw_ref[...]
        lw_hi, lw_lo = _split(lw)
        cum = (jnp.dot(incl1[d], lw_hi, preferred_element_type=jnp.float32)
               + jnp.dot(incl1[d], lw_lo, preferred_element_type=jnp.float32))
        p_in = jnp.exp(cum)
        p_inv = jnp.exp(-cum)
        kk = kk_ref[...]
        return dict(at=-kk * jnp.exp(cum - lw), bt=kk * a_ref[...] * p_inv, kt=kd_ref[...] * p_inv,
                    rt=r_ref[...] * p_in, v=v_ref[...], p_tot=jnp.exp(jnp.sum(lw, axis=0, keepdims=True)))

    q = (scaled(0, lw0_ref, a0_ref, kd0_ref, kk0_ref, v0_ref, r0_ref),
         scaled(1, lw1_ref, a1_ref, kd1_ref, kk1_ref, v1_ref, r1_ref))
    y_refs = (y0_ref, y1_ref)

    chains = [(d, p) for p in range(N_PAIR) for d in range(N_DIR)]
    ids = range(len(chains))
    sl = [slice(p * PAIR_W, (p + 1) * PAIR_W) for _, p in chains]
    atx, btx, ktx, rtx, vx = ([stack(q[d][name][:, sl[i]]) for i, (d, _) in enumerate(chains)]
                              for name in ('at', 'bt', 'kt', 'rt', 'v'))
    big = [_dot_nt(jnp.concatenate([atx[i], rtx[i]], axis=0), jnp.concatenate([btx[i], ktx[i]], axis=0))
           for i in ids]
    a_ab = [jnp.where(strict[chains[i][0]], big[i][:n2, :n2], 0.0) for i in ids]
    a_ak = [jnp.where(strict[chains[i][0]], big[i][:n2, n2:], 0.0) for i in ids]
    a_rb = [jnp.where(incl[chains[i][0]], big[i][n2:, :n2], 0.0) for i in ids]
    a_rk = [jnp.where(incl[chains[i][0]], big[i][n2:, n2:], 0.0) for i in ids]

    tinv = [eye + a_ab[i] for i in ids]
    apow = a_ab
    span = 2
    while span < ck:
        apow = [_dot(apow[i], apow[i]) for i in ids]
        tinv = [tinv[i] + _dot(apow[i], tinv[i]) for i in ids]
        span *= 2
    resid = [(eye - tinv[i]) + _dot3(a_ab[i], tinv[i]) for i in ids]
    tinv = [tinv[i] + _dot(tinv[i], resid[i]) for i in ids]

    av = [_dot(jnp.concatenate([a_ak[i], a_rk[i]], axis=0), vx[i]) for i in ids]
    wu = [_dot(tinv[i], jnp.concatenate([atx[i], av[i][:n2]], axis=1)) for i in ids]
    kv = [_dot(jnp.transpose(vx[i]), ktx[i]) for i in ids]

    s = [s_ref[d, p] for d, p in chains]
    g = [_dot_nt(jnp.concatenate([wu[i][:, :PAIR_W], rtx[i]], axis=0), s[i]) for i in ids]
    u = [g[i][:n2] + wu[i][:, PAIR_W:] for i in ids]
    su = [_dot(jnp.transpose(u[i]), btx[i]) for i in ids]
    yx = [g[i][n2:] + _dot(a_rb[i], u[i]) + av[i][n2:] for i in ids]
    for i, (d, p) in enumerate(chains):
        s_ref[d, p] = (s[i] + su[i] + kv[i]) * q[d]['p_tot'][:, sl[i]]
        y_refs[d][:, sl[i]] = yx[i][:ck] + yx[i][ck:]


def _rwkv_scan(lw, a, kd, kk, v, r, s0):
    _, b_, t_, w_ = lw.shape
    ck = _tile(t_, SCAN_CHUNK)
    nc = t_ // ck
    fwd = pl.BlockSpec((None, None, ck, w_), lambda b, c: (0, b, c, 0))
    bwd = pl.BlockSpec((None, None, ck, w_), lambda b, c: (1, b, nc - 1 - c, 0))
    fwd_shared = pl.BlockSpec((None, ck, w_), lambda b, c: (b, c, 0))
    bwd_shared = pl.BlockSpec((None, ck, w_), lambda b, c: (b, nc - 1 - c, 0))
    state_spec = pl.BlockSpec((N_DIR, None, N_PAIR, PAIR_W, PAIR_W), lambda b, c: (0, b, 0, 0, 0))
    y0, y1, s_fin = pl.pallas_call(
        _scan_kernel,
        grid=(b_, nc),
        in_specs=[fwd, bwd, fwd, bwd, fwd, bwd, fwd_shared, bwd_shared, fwd_shared, bwd_shared,
                  fwd_shared, bwd_shared, state_spec],
        out_specs=[fwd_shared, bwd_shared, state_spec],
        out_shape=[jax.ShapeDtypeStruct((b_, t_, w_), jnp.float32), jax.ShapeDtypeStruct((b_, t_, w_), jnp.float32),
                   jax.ShapeDtypeStruct((N_DIR, b_, N_PAIR, PAIR_W, PAIR_W), jnp.float32)],
        compiler_params=_params("parallel", "arbitrary"),
        name="rwkv_scan",
    )(lw, lw, a, a, kd, kd, kk, kk, v, v, r, r, s0)
    return y0, y1, s_fin


def _rope_tables(n_tok):
    rows = n_tok // GRID_W
    row = jnp.broadcast_to(jnp.arange(rows)[:, None], (rows, GRID_W)).reshape(-1)
    col = jnp.broadcast_to(jnp.arange(GRID_W)[None, :], (rows, GRID_W)).reshape(-1)
    nf = B_HD // 4
    inv = ROPE_BASE ** (-jnp.arange(nf, dtype=jnp.float32) / nf)
    ang_r = row.astype(jnp.float32)[:, None] * inv[None, :]
    ang_c = col.astype(jnp.float32)[:, None] * inv[None, :]
    cos = jnp.concatenate([jnp.cos(ang_r)] * 2 + [jnp.cos(ang_c)] * 2, axis=1)
    sin = jnp.concatenate([-jnp.sin(ang_r), jnp.sin(ang_r), -jnp.sin(ang_c), jnp.sin(ang_c)], axis=1)
    return jnp.tile(cos, (1, 2)), jnp.tile(sin, (1, 2))


def _layer_tables(p, lam_init):
    w = _bf(p['w_in'])
    off, cols = 0, {}
    for name, size in zip(('dk', 'dv', 'kr', 'vr', 'wl', 'al', 'dq', 'r', 'u', 'va', 'za', 'zb', 'zc', 'gl'),
                          STATE_SIZES + OUT_SIZES):
        cols[name] = (off, off + size)
        off += size
    sl = lambda a, b=None: w[:, cols[a][0]:cols[b or a][1]]
    grp = jnp.arange(BR_W) // C_HD
    ones64 = _bf((grp[:, None] == grp[None, :]).astype(jnp.float32))
    lora_rows = jnp.arange(N_DIR * W_LORA) // W_LORA

    def padded(w2):
        stacked = jnp.concatenate([w2, w2], axis=1)
        return _bf(jnp.where((lora_rows[None, :, None] == jnp.arange(N_DIR)[:, None, None]), stacked, 0.0))

    row = lambda v: v.reshape(1, -1).astype(jnp.float32)
    return dict(
        wk=sl('dk'), wv=sl('dv'), wq=sl('dq'), ones64=ones64, mean64=_bf(ones64.astype(jnp.float32) / C_HD),
        gk=row(jnp.tile(p['d_knorm'], BR_W // B_HD)), gq=row(jnp.tile(p['d_qnorm'], BR_W // B_HD)),
        wkr=sl('kr'), wvr=sl('vr'), wr=sl('r'), wla=sl('wl', 'al'),
        w2=padded(p['r_w2']), a2=padded(p['r_a2']),
        w0=p['r_w0'].reshape(N_DIR, 1, BR_W), a0=p['r_a0'].reshape(N_DIR, 1, BR_W),
        wgates=sl('u', 'zc'), a_ln_g=row(p['a_ln_g']), a_ln_b=row(p['a_ln_b']),
        wgl=sl('gl'), wbr=_bf(p['w_br']), wout=_bf(p['w_out']),
        r_ln_g=row(p['r_ln_g']), r_ln_b=row(p['r_ln_b']),
        subln_g=row(p['d_subln_g'] * (1.0 - lam_init)),
        a_ws=_bf(p['a_ws']),
        a_bias=jnp.repeat(jnp.swapaxes(p['a_bs'], 0, 1), A_GW, axis=1).astype(jnp.float32),
    )


def _token_shift(x, taps):
    xp = jnp.pad(x, ((0, 0), (1, 1), (0, 0)))
    return xp[:, :-2] * taps[0] + xp[:, 1:-1] * taps[1] + xp[:, 2:] * taps[2]


def _stream(x, gs, sh, gate, lw, p, rope, lam, ctx_kv, s0, need_out):
    b_, t_, d = x.shape
    n = b_ * t_
    x2 = x.reshape(n, d)
    k_att, v_att, q = _qkv_proj(x2, t_, gs, sh, lw, rope)
    kr, vr, r, lwd, a = _rwkv_in_proj(x2, t_, gs, sh, lw)
    as3 = lambda z: z.reshape(b_, t_, BR_W)
    k_att, v_att = as3(k_att), as3(v_att)
    k = _token_shift(as3(kr), p['r_conv'][1])
    v = _token_shift(as3(vr), p['r_conv'][2])
    r = _token_shift(as3(r), p['r_conv'][0])
    lwd = lwd.reshape(N_DIR, b_, t_, BR_W)
    a = a.reshape(N_DIR, b_, t_, BR_W)
    kk = (k * p['r_kk']).reshape(b_, t_, C_HEADS, C_HD)
    kk = (kk / jnp.maximum(jnp.sqrt(jnp.sum(kk * kk, axis=-1, keepdims=True)), 1e-12)).reshape(b_, t_, BR_W)
    k_dir = k[None] * (1.0 + (a - 1.0) * p['r_ka'])
    y0, y1, s_fin = _rwkv_scan(lwd, a, k_dir, kk, v, r, s0)
    if not need_out:
        return None, (k_att, v_att), s_fin
    hd = lambda z: z.reshape(b_, t_, C_HEADS, C_HD)
    bonus = (jnp.sum(hd(r) * hd(k) * p['r_rk'], axis=-1, keepdims=True) * hd(v)).reshape(n, BR_W)
    uz, vn, szb, szc = _gates_proj(x2, t_, gs, sh, lw)
    ya = _gmlp_branch(vn, uz, lw['a_ws'], lw['a_bias'])
    ks = [k_att] + ([ctx_kv[0]] if ctx_kv is not None else [])
    vs = [v_att] + ([ctx_kv[1]] if ctx_kv is not None else [])
    yb = _diff_attention(as3(q), as3(szb), lw['subln_g'], ks, vs, lam).reshape(n, BR_W)
    out = _merge(x2, t_, gs, sh, gate, ya, yb, y0.reshape(n, BR_W), y1.reshape(n, BR_W), bonus, szc, lw)
    return out.reshape(b_, t_, d), (k_att, v_att), s_fin


def _layer(x, xc, c_act, cc_act, rope, lam_init, p, update_ctx):
    d = D_MODEL
    b_ = x.shape[0]
    lw = _layer_tables(p, lam_init)
    mod_all = _mod_rows(jnp.concatenate([c_act, cc_act[None]], axis=0), p['w_mod'], p['b_mod'])
    mod, mod_c = mod_all[:b_], mod_all[b_:]
    rows3 = lambda z: z[:, None, :]
    g = p['norm_g']
    lp = p['d_lam']
    lam = jnp.exp(jnp.sum(lp[0] * lp[1])) - jnp.exp(jnp.sum(lp[2] * lp[3])) + lam_init
    s_zero = jnp.zeros((N_DIR, b_, N_PAIR, PAIR_W, PAIR_W), jnp.float32)

    xc_next, ctx_kv, s_ctx = _stream(xc, rows3(g * (1.0 + mod_c[:, d:2 * d])), rows3(mod_c[:, :d]),
                                     rows3(mod_c[:, 2 * d:]), lw, p, None, lam, None, s_zero, update_ctx)
    x_next, _, _ = _stream(x, rows3(g * (1.0 + mod[:, d:2 * d])), rows3(mod[:, :d]), rows3(mod[:, 2 * d:]),
                           lw, p, rope, lam, ctx_kv, s_ctx, True)
    return x_next, xc_next


def kernel(x, c, ctx, c_ctx, w_mod, b_mod, norm_g, w_in, a_ln_g, a_ln_b, a_ws, a_bs, d_qnorm, d_knorm, d_lam,
           d_subln_g, r_conv, r_w0, r_w2, r_a0, r_a2, r_kk, r_ka, r_rk, r_ln_g, r_ln_b, w_br, w_out):
    rope = _rope_tables(x.shape[1])
    c_act = jax.nn.silu(c)
    cc_act = jax.nn.silu(c_ctx)
    xc = ctx
    depth = w_in.shape[0]
    for l in range(depth):
        p = dict(w_mod=w_mod[l], b_mod=b_mod[l], norm_g=norm_g[l], w_in=w_in[l],
                 a_ln_g=a_ln_g[l], a_ln_b=a_ln_b[l], a_ws=a_ws[l], a_bs=a_bs[l],
                 d_qnorm=d_qnorm[l], d_knorm=d_knorm[l], d_lam=d_lam[l], d_subln_g=d_subln_g[l],
                 r_conv=r_conv[l], r_w0=r_w0[l], r_w2=r_w2[l], r_a0=r_a0[l], r_a2=r_a2[l],
                 r_kk=r_kk[l], r_ka=r_ka[l], r_rk=r_rk[l], r_ln_g=r_ln_g[l], r_ln_b=r_ln_b[l],
                 w_br=w_br[l], w_out=w_out[l])
        lam_init = 0.8 - 0.6 * math.exp(-0.3 * l)
        x, xc = _layer(x, xc, c_act, cc_act, rope, lam_init, p, l < depth - 1)
    return x
```

```python
import functools
import math

import jax
import jax.numpy as jnp
from jax import lax
from jax.experimental import pallas as pl
from jax.experimental.pallas import tpu as pltpu

D_MODEL = 1024
GRID_W = 64
N_BRANCH = 3
BR_W = D_MODEL // 2
A_GROUPS = 4
A_GW = BR_W // A_GROUPS
A_CHUNK = 128
B_HD = 64
B_VD = 2 * B_HD
B_HEADS = BR_W // B_VD
ROPE_BASE = 10000.0
C_HD = 64
C_HEADS = BR_W // C_HD
N_DIR = 2
W_LORA = 64
A_LORA = 64
NORM_EPS = 1e-6
LN_EPS = 1e-5
GN_EPS = 64e-5
STATE_SIZES = (BR_W, BR_W, BR_W, BR_W, N_DIR * W_LORA, N_DIR * A_LORA)
OUT_SIZES = (BR_W,) * 7 + (N_BRANCH * D_MODEL,)
STATE_COLS = sum(STATE_SIZES)
OUT_COLS = sum(OUT_SIZES)

ROW_TILE = 256
HALO_ROWS = 16
SCAN_CHUNK = 64
ATTN_TQ = 512
ATTN_ROWS = 256
ATTN_UNROLL = 8
ATTN_SAFE_BOUND = 60.0
VMEM_LIMIT_BYTES = 48 * 1024 * 1024
LOG2E = 1.4426950408889634
DECAY_SCALE = math.exp(-0.5)


def _tile(n, pref):
    if n <= pref:
        return n
    t = pref
    while n % t:
        t //= 2
    return t


def _bf(x):
    return x.astype(jnp.bfloat16)


def _dot(a, b):
    return jnp.dot(_bf(a), _bf(b), preferred_element_type=jnp.float32)


def _dot_nt(a, b):
    return lax.dot_general(_bf(a), _bf(b), (((1,), (1,)), ((), ())), preferred_element_type=jnp.float32)


def _split(x):
    hi = _bf(x)
    return hi, _bf(x - hi.astype(jnp.float32))


def _dot3(a, b):
    ah, al = _split(a)
    bh, bl = _split(b)
    f = lambda x, y: jnp.dot(x, y, preferred_element_type=jnp.float32)
    return f(ah, bh) + (f(al, bh) + f(ah, bl))


def _group_sum(x, ones_ref):
    hi, lo = _split(x)
    g = ones_ref[...]
    return (jnp.dot(hi, g, preferred_element_type=jnp.float32) + jnp.dot(lo, g, preferred_element_type=jnp.float32))


def _params(*semantics):
    return pltpu.CompilerParams(dimension_semantics=semantics, vmem_limit_bytes=VMEM_LIMIT_BYTES)


def _mod_kernel(a_ref, w_ref, b_ref, o_ref):
    o_ref[...] = jnp.dot(a_ref[...], w_ref[...], preferred_element_type=jnp.float32) + b_ref[...]


def _mod_rows(act, w_mod, b_mod):
    r = act.shape[0]
    a = _bf(jnp.pad(act, ((0, -r % 16), (0, 0))))
    w = _bf(w_mod)
    b = b_mod.reshape(1, -1)
    out = pl.pallas_call(
        _mod_kernel,
        grid=(1,),
        in_specs=[_const_spec(a), _const_spec(w), _const_spec(b)],
        out_specs=pl.BlockSpec((a.shape[0], w.shape[1]), lambda i: (0, 0)),
        out_shape=jax.ShapeDtypeStruct((a.shape[0], w.shape[1]), jnp.float32),
        compiler_params=_params("arbitrary"), name="mod_rows",
    )(a, w, b)
    return out[:r]


def _modulated(x, gs, sh):
    rinv = lax.rsqrt(jnp.mean(x * x, axis=-1, keepdims=True) + NORM_EPS)
    return _bf(x * rinv * gs + sh)


def _qkv_kernel(x_ref, gs_ref, sh_ref, wk_ref, wv_ref, wq_ref, ones_ref, gk_ref, gq_ref, *rest, rope):
    if rope:
        cos_ref, sin_ref, k_out, v_out, q_out = rest
    else:
        k_out, v_out, q_out = rest
    h = _modulated(x_ref[...], gs_ref[...], sh_ref[...])

    if rope:
        cos = jnp.concatenate([cos_ref[...]] * B_HEADS, axis=1)
        sin = jnp.concatenate([sin_ref[...]] * B_HEADS, axis=1)
        lane = lax.broadcasted_iota(jnp.int32, (1, BR_W), 1)
        first = (lane % (B_HD // 2)) < (B_HD // 4)

    def normed(z, g_ref):
        ss = _group_sum(z * z, ones_ref)
        y = z * lax.rsqrt(ss * (1.0 / B_HD) + NORM_EPS) * g_ref[...]
        if rope:
            partner = jnp.where(first, pltpu.roll(y, BR_W - B_HD // 4, 1), pltpu.roll(y, B_HD // 4, 1))
            y = y * cos + partner * sin
        return y

    k_out[...] = _bf(normed(jnp.dot(h, wk_ref[...], preferred_element_type=jnp.float32), gk_ref))
    v_out[...] = _bf(jnp.dot(h, wv_ref[...], preferred_element_type=jnp.float32))
    q_out[...] = _bf(normed(jnp.dot(h, wq_ref[...], preferred_element_type=jnp.float32), gq_ref)
                     * (B_HD ** -0.5 * LOG2E))


def _rwkv_in_kernel(x_ref, xp_ref, xn_ref, gs_ref, sh_ref, w_ref, wla_ref, w2_ref, a2_ref, w0_ref, a0_ref,
                    taps_ref, rkk_ref, rka_ref, rrk_ref, ones_ref,
                    kk_out, v_out, r_out, kd_out, lw_out, a_out, bonus_out, *, per_batch):
    i = pl.program_id(0)
    gs, sh = gs_ref[...], sh_ref[...]
    h = _modulated(x_ref[...], gs, sh)
    w = w_ref[...]
    z = jnp.dot(h, w, preferred_element_type=jnp.float32)
    tm = z.shape[0]
    zp = jnp.dot(_modulated(xp_ref[...], gs, sh), w, preferred_element_type=jnp.float32)[HALO_ROWS - 1:HALO_ROWS]
    zn = jnp.dot(_modulated(xn_ref[...], gs, sh), w, preferred_element_type=jnp.float32)[0:1]
    zp = jnp.where(i % per_batch == 0, 0.0, zp)
    zn = jnp.where(i % per_batch == per_batch - 1, 0.0, zn)
    rowi = lax.broadcasted_iota(jnp.int32, (tm, 1), 0)
    before = jnp.where(rowi == 0, zp, pltpu.roll(z, 1, 0))
    after = jnp.where(rowi == tm - 1, zn, pltpu.roll(z, tm - 1, 0))
    z = before * taps_ref[0] + z * taps_ref[1] + after * taps_ref[2]
    k, v, r = z[:, :BR_W], z[:, BR_W:2 * BR_W], z[:, 2 * BR_W:]
    v_out[...] = v
    r_out[...] = r
    kk = k * rkk_ref[...]
    kk_out[...] = kk / jnp.maximum(jnp.sqrt(_group_sum(kk * kk, ones_ref)), 1e-12)
    bonus_out[...] = _group_sum(r * k * rrk_ref[...], ones_ref) * v

    la = jnp.dot(h, wla_ref[...], preferred_element_type=jnp.float32)
    n_l = N_DIR * W_LORA
    wl = _bf(jnp.tanh(la[:, :n_l]))
    al = _bf(la[:, n_l:])
    for d in range(N_DIR):
        w_log = w0_ref[d] + jnp.dot(wl, w2_ref[d], preferred_element_type=jnp.float32)
        lw_out[d] = -DECAY_SCALE * jax.nn.sigmoid(w_log)
        a = jax.nn.sigmoid(a0_ref[d] + jnp.dot(al, a2_ref[d], preferred_element_type=jnp.float32))
        a_out[d] = a
        kd_out[d] = k * (1.0 + (a - 1.0) * rka_ref[...])


def _gates_kernel(x_ref, gs_ref, sh_ref, w_ref, lng_ref, lnb_ref, uz_out, vn_out, szb_out, szc_out):
    h = _modulated(x_ref[...], gs_ref[...], sh_ref[...])
    z = jnp.dot(h, w_ref[...], preferred_element_type=jnp.float32)
    u, va, za, zb, zc = (z[:, i * BR_W:(i + 1) * BR_W] for i in range(5))
    uz_out[...] = _bf(jax.nn.gelu(u) * jax.nn.silu(za))
    va = jax.nn.gelu(va)
    mu = jnp.mean(va, axis=-1, keepdims=True)
    var = jnp.mean(jnp.square(va - mu), axis=-1, keepdims=True)
    vn_out[...] = _bf((va - mu) * lax.rsqrt(var + LN_EPS) * lng_ref[...] + lnb_ref[...])
    szb_out[...] = _bf(jax.nn.silu(zb))
    szc_out[...] = _bf(jax.nn.silu(zc))


def _row_specs(n, t_, n_mod):
    tm = _tile(t_, ROW_TILE)
    per_batch = t_ // tm
    x_spec = pl.BlockSpec((tm, D_MODEL), lambda i: (i, 0))
    mod_spec = pl.BlockSpec((None, 1, D_MODEL), (lambda i: (i // per_batch, 0, 0)) if n_mod > 1 else (lambda i: (0, 0, 0)))
    return tm, per_batch, x_spec, mod_spec


def _const_spec(a):
    nd = a.ndim
    return pl.BlockSpec(a.shape, lambda i: (0,) * nd)


def _out_spec(tm, w):
    return pl.BlockSpec((tm, w), lambda i: (i, 0))


def _qkv_proj(x2, t_, gs, sh, lw, rope):
    n = x2.shape[0]
    tm, per_batch, x_spec, mod_spec = _row_specs(n, t_, gs.shape[0])
    consts = [lw['wk'], lw['wv'], lw['wq'], lw['ones64'], lw['gk'], lw['gq']]
    in_specs = [x_spec, mod_spec, mod_spec] + [_const_spec(a) for a in consts]
    args = [x2, gs, sh] + consts
    if rope is not None:
        rope_spec = pl.BlockSpec((tm, B_VD), lambda i: (i % per_batch, 0))
        in_specs += [rope_spec, rope_spec]
        args += list(rope)
    out = jax.ShapeDtypeStruct((n, BR_W), jnp.bfloat16)
    return pl.pallas_call(
        functools.partial(_qkv_kernel, rope=rope is not None),
        grid=(n // tm,), in_specs=in_specs, out_specs=[_out_spec(tm, BR_W)] * 3, out_shape=[out] * 3,
        compiler_params=_params("parallel"), name="qkv_proj",
    )(*args)


def _rwkv_in_proj(x2, t_, gs, sh, lw):
    n = x2.shape[0]
    tm, per_batch, x_spec, mod_spec = _row_specs(n, t_, gs.shape[0])
    halo_per_tile = tm // HALO_ROWS
    prev_spec = pl.BlockSpec((HALO_ROWS, D_MODEL), lambda i: (jnp.maximum(i * halo_per_tile - 1, 0), 0))
    next_spec = pl.BlockSpec((HALO_ROWS, D_MODEL),
                             lambda i: (jnp.minimum((i + 1) * halo_per_tile, n // HALO_ROWS - 1), 0))
    consts = [lw['wkvr'], lw['wla'], lw['w2'], lw['a2'], lw['w0'], lw['a0'], lw['taps'], lw['r_kk'], lw['r_ka'],
              lw['r_rk'], lw['ones64']]
    f32 = jax.ShapeDtypeStruct((n, BR_W), jnp.float32)
    f32d = jax.ShapeDtypeStruct((N_DIR, n, BR_W), jnp.float32)
    row_spec = _out_spec(tm, BR_W)
    dir_spec = pl.BlockSpec((N_DIR, tm, BR_W), lambda i: (0, i, 0))
    return pl.pallas_call(
        functools.partial(_rwkv_in_kernel, per_batch=per_batch),
        grid=(n // tm,),
        in_specs=[x_spec, prev_spec, next_spec, mod_spec, mod_spec] + [_const_spec(a) for a in consts],
        out_specs=[row_spec] * 3 + [dir_spec] * 3 + [row_spec],
        out_shape=[f32] * 3 + [f32d] * 3 + [f32],
        compiler_params=_params("parallel"), name="rwkv_in_proj",
    )(x2, x2, x2, gs, sh, *consts)


def _gates_proj(x2, t_, gs, sh, lw):
    n = x2.shape[0]
    tm, _, x_spec, mod_spec = _row_specs(n, t_, gs.shape[0])
    consts = [lw['wgates'], lw['a_ln_g'], lw['a_ln_b']]
    out = jax.ShapeDtypeStruct((n, BR_W), jnp.bfloat16)
    return pl.pallas_call(
        _gates_kernel,
        grid=(n // tm,), in_specs=[x_spec, mod_spec, mod_spec] + [_const_spec(a) for a in consts],
        out_specs=[_out_spec(tm, BR_W)] * 4, out_shape=[out] * 4,
        compiler_params=_params("parallel"), name="gates_proj",
    )(x2, gs, sh, *consts)


def _merge_kernel(x_ref, gs_ref, sh_ref, gate_ref, ya_ref, yb_ref, y0_ref, y1_ref, bonus_ref, szc_ref,
                  mean_ref, lng_ref, lnb_ref, wgl_ref, wbr_ref, wout_ref, o_ref):
    x = x_ref[...]
    h = _modulated(x_ref[...], gs_ref[...], sh_ref[...])
    g = jax.nn.sigmoid(jnp.dot(h, wgl_ref[...], preferred_element_type=jnp.float32))
    y = y0_ref[...] + y1_ref[...]
    dev = y - _group_sum(y, mean_ref)
    var = _group_sum(dev * dev, mean_ref)
    yc = (dev * lax.rsqrt(var + GN_EPS) * lng_ref[...] + lnb_ref[...] + bonus_ref[...]) * szc_ref[...].astype(jnp.float32)
    ys = (ya_ref[...], yb_ref[...], _bf(yc))
    mix = None
    for i in range(N_BRANCH):
        up = jnp.dot(ys[i], wbr_ref[i], preferred_element_type=jnp.float32)
        term = g[:, i * D_MODEL:(i + 1) * D_MODEL] * up
        mix = term if mix is None else mix + term
    o_ref[...] = x + gate_ref[...] * jnp.dot(_bf(mix), wout_ref[...], preferred_element_type=jnp.float32)


def _merge(x2, t_, gs, sh, gate, ya, yb, y0, y1, bonus, szc, lw):
    n = x2.shape[0]
    tm, _, x_spec, mod_spec = _row_specs(n, t_, gs.shape[0])
    consts = [lw['mean64'], lw['r_ln_g'], lw['r_ln_b'], lw['wgl'], lw['wbr'], lw['wout']]
    br = _out_spec(tm, BR_W)
    return pl.pallas_call(
        _merge_kernel,
        grid=(n // tm,),
        in_specs=[x_spec, mod_spec, mod_spec, mod_spec] + [br] * 6 + [_const_spec(a) for a in consts],
        out_specs=_out_spec(tm, D_MODEL), out_shape=jax.ShapeDtypeStruct((n, D_MODEL), jnp.float32),
        compiler_params=_params("parallel"), name="merge_out",
    )(x2, gs, sh, gate, ya, yb, y0, y1, bonus, szc, *consts)


def _attn_kernel(lam_ref, q_ref, szb_ref, g_ref, *refs, chunks):
    n_src = len(chunks)
    k_refs = refs[:n_src]
    v_refs = refs[n_src:2 * n_src]
    o_ref = refs[2 * n_src]
    kmax_scr = refs[2 * n_src + 1]
    q = q_ref[0]
    tq = q.shape[0]
    rows = min(tq, ATTN_ROWS)
    lane = lax.broadcasted_iota(jnp.int32, (1, B_VD), 1)
    sub = (lane < B_HD, lane >= B_HD)
    zero = jnp.zeros_like(q)
    q_sub = [jnp.where(mask, q, zero) for mask in sub]
    qs = [qj[r:r + rows] for r in range(0, tq, rows) for qj in q_sub]

    @pl.when(pl.program_id(2) == 0)
    def _():
        for j, mask in enumerate(sub):
            best = jnp.zeros((1, 1), jnp.float32)
            for k_ref in k_refs:
                kf = k_ref[0].astype(jnp.float32)
                norm2 = jnp.sum(jnp.where(mask, kf * kf, 0.0), axis=-1, keepdims=True)
                best = jnp.maximum(best, jnp.max(norm2, axis=0, keepdims=True))
            kmax_scr[j] = jnp.broadcast_to(best, kmax_scr.shape[1:])

    qf = q.astype(jnp.float32)
    bound_sub = [jnp.sqrt(jnp.sum(jnp.where(mask, qf * qf, 0.0), axis=-1, keepdims=True) * kmax_scr[j][0:1, 0:1])
                 for j, mask in enumerate(sub)]
    bounds = [bj[r:r + rows] for r in range(0, tq, rows) for bj in bound_sub]
    worst = jnp.max(jnp.maximum(bound_sub[0], bound_sub[1]))

    def scores(k):
        return [lax.dot_general(qj, k, (((1,), (1,)), ((), ())), preferred_element_type=jnp.float32) for qj in qs]

    def fixed_update(carry, k, v):
        ps = [jnp.exp2(s - b) for s, b in zip(scores(k), bounds)]
        pv = [jnp.dot(p.astype(v.dtype), v, preferred_element_type=jnp.float32) for p in ps]
        return tuple((l + jnp.sum(p, axis=-1, keepdims=True), acc + o) for (l, acc), p, o in zip(carry, ps, pv))

    def online_update(carry, k, v):
        ss = scores(k)
        m_new = [jnp.maximum(m, jnp.max(s, axis=-1, keepdims=True)) for s, (m, _, _) in zip(ss, carry)]
        ps = [jnp.exp2(s - mn) for s, mn in zip(ss, m_new)]
        pv = [jnp.dot(p.astype(v.dtype), v, preferred_element_type=jnp.float32) for p in ps]
        new = []
        for j, (m, l, acc) in enumerate(carry):
            alpha = jnp.exp2(m - m_new[j])
            new.append((m_new[j], alpha * l + jnp.sum(ps[j], axis=-1, keepdims=True), alpha * acc + pv[j]))
        return tuple(new)

    def over_keys(update, carry):
        for k_ref, v_ref, (n_chunk, tk) in zip(k_refs, v_refs, chunks):
            def body(i, c, k_ref=k_ref, v_ref=v_ref, tk=tk):
                off = pl.multiple_of(i * tk, tk)
                return update(c, k_ref[0, pl.ds(off, tk), :], v_ref[0, pl.ds(off, tk), :])
            carry = lax.fori_loop(0, n_chunk, body, carry, unroll=math.gcd(n_chunk, ATTN_UNROLL))
        return carry

    def finish(sums):
        lam = lam_ref[0]
        for b in range(tq // rows):
            (l1, acc1), (l2, acc2) = sums[2 * b], sums[2 * b + 1]
            att = acc1 / l1 - lam * (acc2 / l2)
            att = att * lax.rsqrt(jnp.mean(att * att, axis=-1, keepdims=True) + NORM_EPS) * g_ref[...]
            o_ref[0, b * rows:(b + 1) * rows, :] = _bf(att * szb_ref[0, b * rows:(b + 1) * rows, :].astype(jnp.float32))

    zeros = lambda w: jnp.zeros((rows, w), jnp.float32)

    @pl.when(worst <= ATTN_SAFE_BOUND)
    def _():
        finish(over_keys(fixed_update, tuple((zeros(1), zeros(B_VD)) for _ in qs)))

    @pl.when(worst > ATTN_SAFE_BOUND)
    def _():
        carry = over_keys(online_update, tuple((jnp.full((rows, 1), -jnp.inf, jnp.float32), zeros(1), zeros(B_VD))
                                               for _ in qs))
        finish([(l, acc) for _, l, acc in carry])


def _diff_attention(q, szb, subln_g, ks, vs, lam):
    b_, t_, _ = q.shape
    tq = _tile(t_, ATTN_TQ)
    chunks = tuple((k.shape[1] // _tile(k.shape[1], 512), _tile(k.shape[1], 512)) for k in ks)
    kv_specs = [pl.BlockSpec((1, k.shape[1], B_VD), lambda b, h, i: (b, 0, h)) for k in ks]
    q_spec = pl.BlockSpec((1, tq, B_VD), lambda b, h, i: (b, i, h))
    return pl.pallas_call(
        functools.partial(_attn_kernel, chunks=chunks),
        grid=(b_, B_HEADS, t_ // tq),
        in_specs=[pl.BlockSpec(memory_space=pltpu.SMEM), q_spec, q_spec,
                  pl.BlockSpec((1, B_VD), lambda b, h, i: (0, 0))] + kv_specs + kv_specs,
        out_specs=q_spec,
        out_shape=jax.ShapeDtypeStruct((b_, t_, BR_W), jnp.bfloat16),
        scratch_shapes=[pltpu.VMEM((2, 8, B_VD), jnp.float32)],
        compiler_params=_params("parallel", "parallel", "arbitrary"),
        name="diff_attention",
    )(lam.reshape(1).astype(jnp.float32), q, szb, subln_g, *ks, *vs)


def _gmlp_kernel(v_ref, uz_ref, ws_ref, bias_ref, o_ref):
    rows = v_ref.shape[0]
    for n in range(rows // A_CHUNK):
        r0 = n * A_CHUNK
        for g in range(A_GROUPS):
            c0 = g * A_GW
            s = jnp.dot(ws_ref[g], v_ref[r0:r0 + A_CHUNK, c0:c0 + A_GW], preferred_element_type=jnp.float32)
            s = s + bias_ref[:, c0:c0 + A_GW]
            o_ref[r0:r0 + A_CHUNK, c0:c0 + A_GW] = _bf(uz_ref[r0:r0 + A_CHUNK, c0:c0 + A_GW].astype(jnp.float32) * s)


def _gmlp_branch(vn, uz, ws, bias):
    n = vn.shape[0]
    rows = _tile(n, 512)
    row_spec = pl.BlockSpec((rows, BR_W), lambda i: (i, 0))
    return pl.pallas_call(
        _gmlp_kernel,
        grid=(n // rows,),
        in_specs=[row_spec, row_spec, _const_spec(ws), _const_spec(bias)],
        out_specs=row_spec,
        out_shape=jax.ShapeDtypeStruct((n, BR_W), jnp.bfloat16),
        compiler_params=_params("parallel"),
        name="gmlp_mix",
    )(vn, uz, ws, bias)


N_PAIR = C_HEADS // 2
PAIR_W = 2 * C_HD


def _scan_kernel(lw0_ref, lw1_ref, a0_ref, a1_ref, kd0_ref, kd1_ref, kk0_ref, kk1_ref, v0_ref, v1_ref,
                 r0_ref, r1_ref, s0_ref, y0_ref, y1_ref, s_ref):
    @pl.when(pl.program_id(1) == 0)
    def _():
        s_ref[...] = s0_ref[...]

    ck = lw0_ref.shape[0]
    n2 = 2 * ck
    row = lax.broadcasted_iota(jnp.int32, (n2, n2), 0)
    col = lax.broadcasted_iota(jnp.int32, (n2, n2), 1)
    gap = row % ck - col % ck
    strict = (gap > 0, gap < 0)
    incl = (gap >= 0, gap <= 0)
    eye = (row == col).astype(jnp.float32)
    gap1 = lax.broadcasted_iota(jnp.int32, (ck, ck), 0) - lax.broadcasted_iota(jnp.int32, (ck, ck), 1)
    incl1 = (_bf((gap1 >= 0).astype(jnp.float32)), _bf((gap1 <= 0).astype(jnp.float32)))
    head0 = lax.broadcasted_iota(jnp.int32, (1, PAIR_W), 1) < C_HD

    def stack(x):
        return jnp.concatenate([jnp.where(head0, x, 0.0), jnp.where(head0, 0.0, x)], axis=0)

    def scaled(d, lw_ref, a_ref, kd_ref, kk_ref, v_ref, r_ref):
        lw = lw_ref[...]
        lw_hi, lw_lo = _split(lw)
        cum = (jnp.dot(incl1[d], lw_hi, preferred_element_type=jnp.float32)
               + jnp.dot(incl1[d], lw_lo, preferred_element_type=jnp.float32))
        p_in = jnp.exp(cum)
        p_inv = jnp.exp(-cum)
        kk = kk_ref[...]
        return dict(at=-kk * jnp.exp(cum - lw), bt=kk * a_ref[...] * p_inv, kt=kd_ref[...] * p_inv,
                    rt=r_ref[...] * p_in, v=v_ref[...], p_tot=jnp.exp(jnp.sum(lw, axis=0, keepdims=True)))

    q = (scaled(0, lw0_ref, a0_ref, kd0_ref, kk0_ref, v0_ref, r0_ref),
         scaled(1, lw1_ref, a1_ref, kd1_ref, kk1_ref, v1_ref, r1_ref))
    y_refs = (y0_ref, y1_ref)

    chains = [(d, p) for p in range(N_PAIR) for d in range(N_DIR)]
    ids = range(len(chains))
    sl = [slice(p * PAIR_W, (p + 1) * PAIR_W) for _, p in chains]
    atx, btx, ktx, rtx, vx = ([stack(q[d][name][:, sl[i]]) for i, (d, _) in enumerate(chains)]
                              for name in ('at', 'bt', 'kt', 'rt', 'v'))
    big = [_dot_nt(jnp.concatenate([atx[i], rtx[i]], axis=0), jnp.concatenate([btx[i], ktx[i]], axis=0))
           for i in ids]
    a_ab = [jnp.where(strict[chains[i][0]], big[i][:n2, :n2], 0.0) for i in ids]
    a_ak = [jnp.where(strict[chains[i][0]], big[i][:n2, n2:], 0.0) for i in ids]
    a_rb = [jnp.where(incl[chains[i][0]], big[i][n2:, :n2], 0.0) for i in ids]
    a_rk = [jnp.where(incl[chains[i][0]], big[i][n2:, n2:], 0.0) for i in ids]

    tinv = [eye + a_ab[i] for i in ids]
    apow = a_ab
    span = 2
    while span < ck:
        apow = [_dot(apow[i], apow[i]) for i in ids]
        tinv = [tinv[i] + _dot(apow[i], tinv[i]) for i in ids]
        span *= 2
    resid = [(eye - tinv[i]) + _dot3(a_ab[i], tinv[i]) for i in ids]
    tinv = [tinv[i] + _dot(tinv[i], resid[i]) for i in ids]

    av = [_dot(jnp.concatenate([a_ak[i], a_rk[i]], axis=0), vx[i]) for i in ids]
    wu = [_dot(tinv[i], jnp.concatenate([atx[i], av[i][:n2]], axis=1)) for i in ids]
    kv = [_dot(jnp.transpose(vx[i]), ktx[i]) for i in ids]

    s = [s_ref[d, p] for d, p in chains]
    g = [_dot_nt(jnp.concatenate([wu[i][:, :PAIR_W], rtx[i]], axis=0), s[i]) for i in ids]
    u = [g[i][:n2] + wu[i][:, PAIR_W:] for i in ids]
    su = [_dot(jnp.transpose(u[i]), btx[i]) for i in ids]
    yx = [g[i][n2:] + _dot(a_rb[i], u[i]) + av[i][n2:] for i in ids]
    for i, (d, p) in enumerate(chains):
        s_ref[d, p] = (s[i] + su[i] + kv[i]) * q[d]['p_tot'][:, sl[i]]
        y_refs[d][:, sl[i]] = yx[i][:ck] + yx[i][ck:]


def _rwkv_scan(lw, a, kd, kk, v, r, s0):
    _, b_, t_, w_ = lw.shape
    ck = _tile(t_, SCAN_CHUNK)
    nc = t_ // ck
    fwd = pl.BlockSpec((None, None, ck, w_), lambda b, c: (0, b, c, 0))
    bwd = pl.BlockSpec((None, None, ck, w_), lambda b, c: (1, b, nc - 1 - c, 0))
    fwd_shared = pl.BlockSpec((None, ck, w_), lambda b, c: (b, c, 0))
    bwd_shared = pl.BlockSpec((None, ck, w_), lambda b, c: (b, nc - 1 - c, 0))
    state_spec = pl.BlockSpec((N_DIR, None, N_PAIR, PAIR_W, PAIR_W), lambda b, c: (0, b, 0, 0, 0))
    y0, y1, s_fin = pl.pallas_call(
        _scan_kernel,
        grid=(b_, nc),
        in_specs=[fwd, bwd, fwd, bwd, fwd, bwd, fwd_shared, bwd_shared, fwd_shared, bwd_shared,
                  fwd_shared, bwd_shared, state_spec],
        out_specs=[fwd_shared, bwd_shared, state_spec],
        out_shape=[jax.ShapeDtypeStruct((b_, t_, w_), jnp.float32), jax.ShapeDtypeStruct((b_, t_, w_), jnp.float32),
                   jax.ShapeDtypeStruct((N_DIR, b_, N_PAIR, PAIR_W, PAIR_W), jnp.float32)],
        compiler_params=_params("parallel", "arbitrary"),
        name="rwkv_scan",
    )(lw, lw, a, a, kd, kd, kk, kk, v, v, r, r, s0)
    return y0, y1, s_fin


def _rope_tables(n_tok):
    rows = n_tok // GRID_W
    row = jnp.broadcast_to(jnp.arange(rows)[:, None], (rows, GRID_W)).reshape(-1)
    col = jnp.broadcast_to(jnp.arange(GRID_W)[None, :], (rows, GRID_W)).reshape(-1)
    nf = B_HD // 4
    inv = ROPE_BASE ** (-jnp.arange(nf, dtype=jnp.float32) / nf)
    ang_r = row.astype(jnp.float32)[:, None] * inv[None, :]
    ang_c = col.astype(jnp.float32)[:, None] * inv[None, :]
    cos = jnp.concatenate([jnp.cos(ang_r)] * 2 + [jnp.cos(ang_c)] * 2, axis=1)
    sin = jnp.concatenate([-jnp.sin(ang_r), jnp.sin(ang_r), -jnp.sin(ang_c), jnp.sin(ang_c)], axis=1)
    return jnp.tile(cos, (1, 2)), jnp.tile(sin, (1, 2))


def _layer_tables(p, lam_init):
    w = _bf(p['w_in'])
    off, cols = 0, {}
    for name, size in zip(('dk', 'dv', 'kr', 'vr', 'wl', 'al', 'dq', 'r', 'u', 'va', 'za', 'zb', 'zc', 'gl'),
                          STATE_SIZES + OUT_SIZES):
        cols[name] = (off, off + size)
        off += size
    sl = lambda a, b=None: w[:, cols[a][0]:cols[b or a][1]]
    grp = jnp.arange(BR_W) // C_HD
    ones64 = _bf((grp[:, None] == grp[None, :]).astype(jnp.float32))
    lora_rows = jnp.arange(N_DIR * W_LORA) // W_LORA

    def padded(w2):
        stacked = jnp.concatenate([w2, w2], axis=1)
        return _bf(jnp.where((lora_rows[None, :, None] == jnp.arange(N_DIR)[:, None, None]), stacked, 0.0))

    row = lambda v: v.reshape(1, -1).astype(jnp.float32)
    return dict(
        wk=sl('dk'), wv=sl('dv'), wq=sl('dq'), ones64=ones64, mean64=_bf(ones64.astype(jnp.float32) / C_HD),
        gk=row(jnp.tile(p['d_knorm'], BR_W // B_HD)), gq=row(jnp.tile(p['d_qnorm'], BR_W // B_HD)),
        wkvr=jnp.concatenate([sl('kr'), sl('vr'), sl('r')], axis=1), wla=sl('wl', 'al'),
        taps=jnp.concatenate([p['r_conv'][1], p['r_conv'][2], p['r_conv'][0]], axis=1).reshape(3, 1, 3 * BR_W),
        r_kk=row(p['r_kk']), r_ka=row(p['r_ka']), r_rk=row(p['r_rk']),
        w2=padded(p['r_w2']), a2=padded(p['r_a2']),
        w0=p['r_w0'].reshape(N_DIR, 1, BR_W), a0=p['r_a0'].reshape(N_DIR, 1, BR_W),
        wgates=sl('u', 'zc'), a_ln_g=row(p['a_ln_g']), a_ln_b=row(p['a_ln_b']),
        wgl=sl('gl'), wbr=_bf(p['w_br']), wout=_bf(p['w_out']),
        r_ln_g=row(p['r_ln_g']), r_ln_b=row(p['r_ln_b']),
        subln_g=row(p['d_subln_g'] * (1.0 - lam_init)),
        a_ws=_bf(p['a_ws']),
        a_bias=jnp.repeat(jnp.swapaxes(p['a_bs'], 0, 1), A_GW, axis=1).astype(jnp.float32),
    )


def _stream(x, gs, sh, gate, lw, p, rope, lam, ctx_kv, s0, need_out):
    b_, t_, d = x.shape
    n = b_ * t_
    x2 = x.reshape(n, d)
    k_att, v_att, q = _qkv_proj(x2, t_, gs, sh, lw, rope)
    kk, v, r, kd, lwd, a, bonus = _rwkv_in_proj(x2, t_, gs, sh, lw)
    as3 = lambda z: z.reshape(b_, t_, BR_W)
    as4 = lambda z: z.reshape(N_DIR, b_, t_, BR_W)
    k_att, v_att = as3(k_att), as3(v_att)
    y0, y1, s_fin = _rwkv_scan(as4(lwd), as4(a), as4(kd), as3(kk), as3(v), as3(r), s0)
    if not need_out:
        return None, (k_att, v_att), s_fin
    uz, vn, szb, szc = _gates_proj(x2, t_, gs, sh, lw)
    ya = _gmlp_branch(vn, uz, lw['a_ws'], lw['a_bias'])
    ks = [k_att] + ([ctx_kv[0]] if ctx_kv is not None else [])
    vs = [v_att] + ([ctx_kv[1]] if ctx_kv is not None else [])
    yb = _diff_attention(as3(q), as3(szb), lw['subln_g'], ks, vs, lam).reshape(n, BR_W)
    out = _merge(x2, t_, gs, sh, gate, ya, yb, y0.reshape(n, BR_W), y1.reshape(n, BR_W), bonus, szc, lw)
    return out.reshape(b_, t_, d), (k_att, v_att), s_fin


def _layer(x, xc, c_act, cc_act, rope, lam_init, p, update_ctx):
    d = D_MODEL
    b_ = x.shape[0]
    lw = _layer_tables(p, lam_init)
    mod_all = _mod_rows(jnp.concatenate([c_act, cc_act[None]], axis=0), p['w_mod'], p['b_mod'])
    mod, mod_c = mod_all[:b_], mod_all[b_:]
    rows3 = lambda z: z[:, None, :]
    g = p['norm_g']
    lp = p['d_lam']
    lam = jnp.exp(jnp.sum(lp[0] * lp[1])) - jnp.exp(jnp.sum(lp[2] * lp[3])) + lam_init
    s_zero = jnp.zeros((N_DIR, b_, N_PAIR, PAIR_W, PAIR_W), jnp.float32)

    xc_next, ctx_kv, s_ctx = _stream(xc, rows3(g * (1.0 + mod_c[:, d:2 * d])), rows3(mod_c[:, :d]),
                                     rows3(mod_c[:, 2 * d:]), lw, p, None, lam, None, s_zero, update_ctx)
    x_next, _, _ = _stream(x, rows3(g * (1.0 + mod[:, d:2 * d])), rows3(mod[:, :d]), rows3(mod[:, 2 * d:]),
                           lw, p, rope, lam, ctx_kv, s_ctx, True)
    return x_next, xc_next


def kernel(x, c, ctx, c_ctx, w_mod, b_mod, norm_g, w_in, a_ln_g, a_ln_b, a_ws, a_bs, d_qnorm, d_knorm, d_lam,
           d_subln_g, r_conv, r_w0, r_w2, r_a0, r_a2, r_kk, r_ka, r_rk, r_ln_g, r_ln_b, w_br, w_out):
    rope = _rope_tables(x.shape[1])
    c_act = jax.nn.silu(c)
    cc_act = jax.nn.silu(c_ctx)
    xc = ctx
    depth = w_in.shape[0]
    for l in range(depth):
        p = dict(w_mod=w_mod[l], b_mod=b_mod[l], norm_g=norm_g[l], w_in=w_in[l],
                 a_ln_g=a_ln_g[l], a_ln_b=a_ln_b[l], a_ws=a_ws[l], a_bs=a_bs[l],
                 d_qnorm=d_qnorm[l], d_knorm=d_knorm[l], d_lam=d_lam[l], d_subln_g=d_subln_g[l],
                 r_conv=r_conv[l], r_w0=r_w0[l], r_w2=r_w2[l], r_a0=r_a0[l], r_a2=r_a2[l],
                 r_kk=r_kk[l], r_ka=r_ka[l], r_rk=r_rk[l], r_ln_g=r_ln_g[l], r_ln_b=r_ln_b[l],
                 w_br=w_br[l], w_out=w_out[l])
        lam_init = 0.8 - 0.6 * math.exp(-0.3 * l)
        x, xc = _layer(x, xc, c_act, cc_act, rope, lam_init, p, l < depth - 1)
    return x
```

```python
import functools
import math

import jax
import jax.numpy as jnp
from jax import lax
from jax.experimental import pallas as pl
from jax.experimental.pallas import tpu as pltpu

D_MODEL = 1024
GRID_W = 64
N_BRANCH = 3
BR_W = D_MODEL // 2
A_GROUPS = 4
A_GW = BR_W // A_GROUPS
A_CHUNK = 128
B_HD = 64
B_VD = 2 * B_HD
B_HEADS = BR_W // B_VD
ROPE_BASE = 10000.0
C_HD = 64
C_HEADS = BR_W // C_HD
N_DIR = 2
W_LORA = 64
A_LORA = 64
NORM_EPS = 1e-6
LN_EPS = 1e-5
GN_EPS = 64e-5
STATE_SIZES = (BR_W, BR_W, BR_W, BR_W, N_DIR * W_LORA, N_DIR * A_LORA)
OUT_SIZES = (BR_W,) * 7 + (N_BRANCH * D_MODEL,)
STATE_COLS = sum(STATE_SIZES)
OUT_COLS = sum(OUT_SIZES)

ROW_TILE = 256
HALO_ROWS = 16
SCAN_CHUNK = 64
SCAN_CHUNKS_PER_STEP = 2
ATTN_TQ = 512
ATTN_ROWS = 256
ATTN_UNROLL = 8
ATTN_SAFE_BOUND = 60.0
VMEM_LIMIT_BYTES = 48 * 1024 * 1024
LOG2E = 1.4426950408889634
DECAY_SCALE = math.exp(-0.5)


def _tile(n, pref):
    if n <= pref:
        return n
    t = pref
    while n % t:
        t //= 2
    return t


def _bf(x):
    return x.astype(jnp.bfloat16)


def _dot(a, b):
    return jnp.dot(_bf(a), _bf(b), preferred_element_type=jnp.float32)


def _dot_nt(a, b):
    return lax.dot_general(_bf(a), _bf(b), (((1,), (1,)), ((), ())), preferred_element_type=jnp.float32)


def _split(x):
    hi = _bf(x)
    return hi, _bf(x - hi.astype(jnp.float32))


def _dot3_wide(a, b):
    ah, al = _split(a)
    bh, bl = _split(b)
    n = b.shape[1]
    f = lambda x, y: jnp.dot(x, y, preferred_element_type=jnp.float32)
    wide = f(ah, jnp.concatenate([bh, bl], axis=1))
    return wide[:, :n] + (wide[:, n:] + f(al, bh))


def _group_sum(x, ones_ref):
    hi, lo = _split(x)
    g = ones_ref[...]
    return (jnp.dot(hi, g, preferred_element_type=jnp.float32) + jnp.dot(lo, g, preferred_element_type=jnp.float32))


def _params(*semantics):
    return pltpu.CompilerParams(dimension_semantics=semantics, vmem_limit_bytes=VMEM_LIMIT_BYTES)


def _mod_kernel(a_ref, w_ref, b_ref, o_ref):
    o_ref[...] = jnp.dot(a_ref[...], w_ref[...], preferred_element_type=jnp.float32) + b_ref[...]


def _mod_rows(act, w_mod, b_mod):
    r = act.shape[0]
    a = _bf(jnp.pad(act, ((0, -r % 16), (0, 0))))
    w = _bf(w_mod)
    b = b_mod.reshape(1, -1)
    out = pl.pallas_call(
        _mod_kernel,
        grid=(1,),
        in_specs=[_const_spec(a), _const_spec(w), _const_spec(b)],
        out_specs=pl.BlockSpec((a.shape[0], w.shape[1]), lambda i: (0, 0)),
        out_shape=jax.ShapeDtypeStruct((a.shape[0], w.shape[1]), jnp.float32),
        compiler_params=_params("arbitrary"), name="mod_rows",
    )(a, w, b)
    return out[:r]


def _modulated(x, gs, sh):
    rinv = lax.rsqrt(jnp.mean(x * x, axis=-1, keepdims=True) + NORM_EPS)
    return _bf(x * rinv * gs + sh)


def _qkv_kernel(x_ref, gs_ref, sh_ref, wk_ref, wv_ref, wq_ref, ones_ref, gk_ref, gq_ref, *rest, rope):
    if rope:
        cos_ref, sin_ref, k_out, v_out, q_out = rest
    else:
        k_out, v_out, q_out = rest
    h = _modulated(x_ref[...], gs_ref[...], sh_ref[...])

    if rope:
        cos = jnp.concatenate([cos_ref[...]] * B_HEADS, axis=1)
        sin = jnp.concatenate([sin_ref[...]] * B_HEADS, axis=1)
        lane = lax.broadcasted_iota(jnp.int32, (1, BR_W), 1)
        first = (lane % (B_HD // 2)) < (B_HD // 4)

    def normed(z, g_ref):
        ss = _group_sum(z * z, ones_ref)
        y = z * lax.rsqrt(ss * (1.0 / B_HD) + NORM_EPS) * g_ref[...]
        if rope:
            partner = jnp.where(first, pltpu.roll(y, BR_W - B_HD // 4, 1), pltpu.roll(y, B_HD // 4, 1))
            y = y * cos + partner * sin
        return y

    zk = jnp.dot(h, wk_ref[...], preferred_element_type=jnp.float32)
    zq = jnp.dot(h, wq_ref[...], preferred_element_type=jnp.float32)
    v_out[...] = _bf(jnp.dot(h, wv_ref[...], preferred_element_type=jnp.float32))
    k_out[...] = _bf(normed(zk, gk_ref))
    q_out[...] = _bf(normed(zq, gq_ref) * (B_HD ** -0.5 * LOG2E))


def _rwkv_in_kernel(x_ref, xp_ref, xn_ref, gs_ref, sh_ref, w_ref, wla_ref, w2_ref, a2_ref, w0_ref, a0_ref,
                    taps_ref, rkk_ref, rka_ref, rrk_ref, ones_ref,
                    kk_out, v_out, r_out, kd_out, lw_out, a_out, bonus_out, *, per_batch):
    i = pl.program_id(0)
    gs, sh = gs_ref[...], sh_ref[...]
    h = _modulated(x_ref[...], gs, sh)
    w = w_ref[...]
    zp = jnp.dot(_modulated(xp_ref[...], gs, sh), w, preferred_element_type=jnp.float32)[HALO_ROWS - 1:HALO_ROWS]
    zn = jnp.dot(_modulated(xn_ref[...], gs, sh), w, preferred_element_type=jnp.float32)[0:1]
    zp = jnp.where(i % per_batch == 0, 0.0, zp)
    zn = jnp.where(i % per_batch == per_batch - 1, 0.0, zn)
    la = jnp.dot(h, wla_ref[...], preferred_element_type=jnp.float32)
    n_l = N_DIR * W_LORA
    wl = _bf(jnp.tanh(la[:, :n_l]))
    al = _bf(la[:, n_l:])
    w_log = [w0_ref[d] + jnp.dot(wl, w2_ref[d], preferred_element_type=jnp.float32) for d in range(N_DIR)]
    a_log = [a0_ref[d] + jnp.dot(al, a2_ref[d], preferred_element_type=jnp.float32) for d in range(N_DIR)]
    z = jnp.dot(h, w, preferred_element_type=jnp.float32)
    tm = z.shape[0]
    rowi = lax.broadcasted_iota(jnp.int32, (tm, 1), 0)
    before = jnp.where(rowi == 0, zp, pltpu.roll(z, 1, 0))
    after = jnp.where(rowi == tm - 1, zn, pltpu.roll(z, tm - 1, 0))
    z = before * taps_ref[0] + z * taps_ref[1] + after * taps_ref[2]
    k, v, r = z[:, :BR_W], z[:, BR_W:2 * BR_W], z[:, 2 * BR_W:]
    v_out[...] = v
    r_out[...] = r
    kk = k * rkk_ref[...]
    kk_out[...] = kk / jnp.maximum(jnp.sqrt(_group_sum(kk * kk, ones_ref)), 1e-12)
    bonus_out[...] = _group_sum(r * k * rrk_ref[...], ones_ref) * v

    for d in range(N_DIR):
        lw_out[d] = -DECAY_SCALE * jax.nn.sigmoid(w_log[d])
        a = jax.nn.sigmoid(a_log[d])
        a_out[d] = a
        kd_out[d] = k * (1.0 + (a - 1.0) * rka_ref[...])


def _gates_kernel(x_ref, gs_ref, sh_ref, w_ref, lng_ref, lnb_ref, uz_out, vn_out, szb_out, szc_out):
    h = _modulated(x_ref[...], gs_ref[...], sh_ref[...])
    z = jnp.dot(h, w_ref[...], preferred_element_type=jnp.float32)
    u, va, za, zb, zc = (z[:, i * BR_W:(i + 1) * BR_W] for i in range(5))
    uz_out[...] = _bf(jax.nn.gelu(u) * jax.nn.silu(za))
    va = jax.nn.gelu(va)
    mu = jnp.mean(va, axis=-1, keepdims=True)
    var = jnp.mean(jnp.square(va - mu), axis=-1, keepdims=True)
    vn_out[...] = _bf((va - mu) * lax.rsqrt(var + LN_EPS) * lng_ref[...] + lnb_ref[...])
    szb_out[...] = _bf(jax.nn.silu(zb))
    szc_out[...] = _bf(jax.nn.silu(zc))


def _row_specs(n, t_, n_mod):
    tm = _tile(t_, ROW_TILE)
    per_batch = t_ // tm
    x_spec = pl.BlockSpec((tm, D_MODEL), lambda i: (i, 0))
    mod_spec = pl.BlockSpec((None, 1, D_MODEL), (lambda i: (i // per_batch, 0, 0)) if n_mod > 1 else (lambda i: (0, 0, 0)))
    return tm, per_batch, x_spec, mod_spec


def _const_spec(a):
    nd = a.ndim
    return pl.BlockSpec(a.shape, lambda i: (0,) * nd)


def _out_spec(tm, w):
    return pl.BlockSpec((tm, w), lambda i: (i, 0))


def _qkv_proj(x2, t_, gs, sh, lw, rope):
    n = x2.shape[0]
    tm, per_batch, x_spec, mod_spec = _row_specs(n, t_, gs.shape[0])
    consts = [lw['wk'], lw['wv'], lw['wq'], lw['ones64'], lw['gk'], lw['gq']]
    in_specs = [x_spec, mod_spec, mod_spec] + [_const_spec(a) for a in consts]
    args = [x2, gs, sh] + consts
    if rope is not None:
        rope_spec = pl.BlockSpec((tm, B_VD), lambda i: (i % per_batch, 0))
        in_specs += [rope_spec, rope_spec]
        args += list(rope)
    out = jax.ShapeDtypeStruct((n, BR_W), jnp.bfloat16)
    return pl.pallas_call(
        functools.partial(_qkv_kernel, rope=rope is not None),
        grid=(n // tm,), in_specs=in_specs, out_specs=[_out_spec(tm, BR_W)] * 3, out_shape=[out] * 3,
        compiler_params=_params("parallel"), name="qkv_proj",
    )(*args)


def _rwkv_in_proj(x2, t_, gs, sh, lw):
    n = x2.shape[0]
    tm, per_batch, x_spec, mod_spec = _row_specs(n, t_, gs.shape[0])
    halo_per_tile = tm // HALO_ROWS
    prev_spec = pl.BlockSpec((HALO_ROWS, D_MODEL), lambda i: (jnp.maximum(i * halo_per_tile - 1, 0), 0))
    next_spec = pl.BlockSpec((HALO_ROWS, D_MODEL),
                             lambda i: (jnp.minimum((i + 1) * halo_per_tile, n // HALO_ROWS - 1), 0))
    consts = [lw['wkvr'], lw['wla'], lw['w2'], lw['a2'], lw['w0'], lw['a0'], lw['taps'], lw['r_kk'], lw['r_ka'],
              lw['r_rk'], lw['ones64']]
    f32 = jax.ShapeDtypeStruct((n, BR_W), jnp.float32)
    f32d = jax.ShapeDtypeStruct((N_DIR, n, BR_W), jnp.float32)
    row_spec = _out_spec(tm, BR_W)
    dir_spec = pl.BlockSpec((N_DIR, tm, BR_W), lambda i: (0, i, 0))
    return pl.pallas_call(
        functools.partial(_rwkv_in_kernel, per_batch=per_batch),
        grid=(n // tm,),
        in_specs=[x_spec, prev_spec, next_spec, mod_spec, mod_spec] + [_const_spec(a) for a in consts],
        out_specs=[row_spec] * 3 + [dir_spec] * 3 + [row_spec],
        out_shape=[f32] * 3 + [f32d] * 3 + [f32],
        compiler_params=_params("parallel"), name="rwkv_in_proj",
    )(x2, x2, x2, gs, sh, *consts)


def _gates_proj(x2, t_, gs, sh, lw):
    n = x2.shape[0]
    tm, _, x_spec, mod_spec = _row_specs(n, t_, gs.shape[0])
    consts = [lw['wgates'], lw['a_ln_g'], lw['a_ln_b']]
    out = jax.ShapeDtypeStruct((n, BR_W), jnp.bfloat16)
    return pl.pallas_call(
        _gates_kernel,
        grid=(n // tm,), in_specs=[x_spec, mod_spec, mod_spec] + [_const_spec(a) for a in consts],
        out_specs=[_out_spec(tm, BR_W)] * 4, out_shape=[out] * 4,
        compiler_params=_params("parallel"), name="gates_proj",
    )(x2, gs, sh, *consts)


def _merge_kernel(x_ref, gs_ref, sh_ref, gate_ref, ya_ref, yb_ref, y0_ref, y1_ref, bonus_ref, szc_ref,
                  mean_ref, lng_ref, lnb_ref, wgl_ref, wbr_ref, wout_ref, o_ref):
    x = x_ref[...]
    y = y0_ref[...] + y1_ref[...]
    dev = y - _group_sum(y, mean_ref)
    var = _group_sum(dev * dev, mean_ref)
    yc = (dev * lax.rsqrt(var + GN_EPS) * lng_ref[...] + lnb_ref[...] + bonus_ref[...]) * szc_ref[...].astype(jnp.float32)
    h = _modulated(x, gs_ref[...], sh_ref[...])
    g = jax.nn.sigmoid(jnp.dot(h, wgl_ref[...], preferred_element_type=jnp.float32))
    ys = (ya_ref[...], yb_ref[...], _bf(yc))
    mix = None
    for i in range(N_BRANCH):
        up = jnp.dot(ys[i], wbr_ref[i], preferred_element_type=jnp.float32)
        term = g[:, i * D_MODEL:(i + 1) * D_MODEL] * up
        mix = term if mix is None else mix + term
    o_ref[...] = x + gate_ref[...] * jnp.dot(_bf(mix), wout_ref[...], preferred_element_type=jnp.float32)


def _merge(x2, t_, gs, sh, gate, ya, yb, y0, y1, bonus, szc, lw):
    n = x2.shape[0]
    tm, _, x_spec, mod_spec = _row_specs(n, t_, gs.shape[0])
    consts = [lw['mean64'], lw['r_ln_g'], lw['r_ln_b'], lw['wgl'], lw['wbr'], lw['wout']]
    br = _out_spec(tm, BR_W)
    return pl.pallas_call(
        _merge_kernel,
        grid=(n // tm,),
        in_specs=[x_spec, mod_spec, mod_spec, mod_spec] + [br] * 6 + [_const_spec(a) for a in consts],
        out_specs=_out_spec(tm, D_MODEL), out_shape=jax.ShapeDtypeStruct((n, D_MODEL), jnp.float32),
        compiler_params=_params("parallel"), name="merge_out",
    )(x2, gs, sh, gate, ya, yb, y0, y1, bonus, szc, *consts)


def _attn_kernel(lam_ref, q_ref, szb_ref, g_ref, *refs, chunks):
    n_src = len(chunks)
    k_refs = refs[:n_src]
    v_refs = refs[n_src:2 * n_src]
    o_ref = refs[2 * n_src]
    kmax_scr = refs[2 * n_src + 1]
    q = q_ref[0]
    tq = q.shape[0]
    rows = min(tq, ATTN_ROWS)
    lane = lax.broadcasted_iota(jnp.int32, (1, B_VD), 1)
    sub = (lane < B_HD, lane >= B_HD)
    zero = jnp.zeros_like(q)
    q_sub = [jnp.where(mask, q, zero) for mask in sub]
    qs = [qj[r:r + rows] for r in range(0, tq, rows) for qj in q_sub]

    @pl.when(pl.program_id(2) == 0)
    def _():
        for j, mask in enumerate(sub):
            best = jnp.zeros((1, 1), jnp.float32)
            for k_ref in k_refs:
                kf = k_ref[0].astype(jnp.float32)
                norm2 = jnp.sum(jnp.where(mask, kf * kf, 0.0), axis=-1, keepdims=True)
                best = jnp.maximum(best, jnp.max(norm2, axis=0, keepdims=True))
            kmax_scr[j] = jnp.broadcast_to(best, kmax_scr.shape[1:])

    qf = q.astype(jnp.float32)
    bound_sub = [jnp.sqrt(jnp.sum(jnp.where(mask, qf * qf, 0.0), axis=-1, keepdims=True) * kmax_scr[j][0:1, 0:1])
                 for j, mask in enumerate(sub)]
    bounds = [bj[r:r + rows] for r in range(0, tq, rows) for bj in bound_sub]
    worst = jnp.max(jnp.maximum(bound_sub[0], bound_sub[1]))

    def scores(k):
        return [lax.dot_general(qj, k, (((1,), (1,)), ((), ())), preferred_element_type=jnp.float32) for qj in qs]

    def fixed_update(carry, k, v):
        ps = [jnp.exp2(s - b) for s, b in zip(scores(k), bounds)]
        pv = [jnp.dot(p.astype(v.dtype), v, preferred_element_type=jnp.float32) for p in ps]
        return tuple((l + jnp.sum(p, axis=-1, keepdims=True), acc + o) for (l, acc), p, o in zip(carry, ps, pv))

    def online_update(carry, k, v):
        ss = scores(k)
        m_new = [jnp.maximum(m, jnp.max(s, axis=-1, keepdims=True)) for s, (m, _, _) in zip(ss, carry)]
        ps = [jnp.exp2(s - mn) for s, mn in zip(ss, m_new)]
        pv = [jnp.dot(p.astype(v.dtype), v, preferred_element_type=jnp.float32) for p in ps]
        new = []
        for j, (m, l, acc) in enumerate(carry):
            alpha = jnp.exp2(m - m_new[j])
            new.append((m_new[j], alpha * l + jnp.sum(ps[j], axis=-1, keepdims=True), alpha * acc + pv[j]))
        return tuple(new)

    def over_keys(update, carry):
        for k_ref, v_ref, (n_chunk, tk) in zip(k_refs, v_refs, chunks):
            def body(i, c, k_ref=k_ref, v_ref=v_ref, tk=tk):
                off = pl.multiple_of(i * tk, tk)
                return update(c, k_ref[0, pl.ds(off, tk), :], v_ref[0, pl.ds(off, tk), :])
            carry = lax.fori_loop(0, n_chunk, body, carry, unroll=math.gcd(n_chunk, ATTN_UNROLL))
        return carry

    def finish(sums):
        lam = lam_ref[0]
        for b in range(tq // rows):
            (l1, acc1), (l2, acc2) = sums[2 * b], sums[2 * b + 1]
            att = acc1 / l1 - lam * (acc2 / l2)
            att = att * lax.rsqrt(jnp.mean(att * att, axis=-1, keepdims=True) + NORM_EPS) * g_ref[...]
            o_ref[0, b * rows:(b + 1) * rows, :] = _bf(att * szb_ref[0, b * rows:(b + 1) * rows, :].astype(jnp.float32))

    zeros = lambda w: jnp.zeros((rows, w), jnp.float32)

    @pl.when(worst <= ATTN_SAFE_BOUND)
    def _():
        finish(over_keys(fixed_update, tuple((zeros(1), zeros(B_VD)) for _ in qs)))

    @pl.when(worst > ATTN_SAFE_BOUND)
    def _():
        carry = over_keys(online_update, tuple((jnp.full((rows, 1), -jnp.inf, jnp.float32), zeros(1), zeros(B_VD))
                                               for _ in qs))
        finish([(l, acc) for _, l, acc in carry])


def _diff_attention(q, szb, subln_g, ks, vs, lam):
    b_, t_, _ = q.shape
    tq = _tile(t_, ATTN_TQ)
    chunks = tuple((k.shape[1] // _tile(k.shape[1], 512), _tile(k.shape[1], 512)) for k in ks)
    kv_specs = [pl.BlockSpec((1, k.shape[1], B_VD), lambda b, h, i: (b, 0, h)) for k in ks]
    q_spec = pl.BlockSpec((1, tq, B_VD), lambda b, h, i: (b, i, h))
    return pl.pallas_call(
        functools.partial(_attn_kernel, chunks=chunks),
        grid=(b_, B_HEADS, t_ // tq),
        in_specs=[pl.BlockSpec(memory_space=pltpu.SMEM), q_spec, q_spec,
                  pl.BlockSpec((1, B_VD), lambda b, h, i: (0, 0))] + kv_specs + kv_specs,
        out_specs=q_spec,
        out_shape=jax.ShapeDtypeStruct((b_, t_, BR_W), jnp.bfloat16),
        scratch_shapes=[pltpu.VMEM((2, 8, B_VD), jnp.float32)],
        compiler_params=_params("parallel", "parallel", "arbitrary"),
        name="diff_attention",
    )(lam.reshape(1).astype(jnp.float32), q, szb, subln_g, *ks, *vs)


def _gmlp_kernel(v_ref, uz_ref, ws_ref, bias_ref, o_ref):
    rows = v_ref.shape[0]
    for n in range(rows // A_CHUNK):
        r0 = n * A_CHUNK
        for g in range(A_GROUPS):
            c0 = g * A_GW
            s = jnp.dot(ws_ref[g], v_ref[r0:r0 + A_CHUNK, c0:c0 + A_GW], preferred_element_type=jnp.float32)
            s = s + bias_ref[:, c0:c0 + A_GW]
            o_ref[r0:r0 + A_CHUNK, c0:c0 + A_GW] = _bf(uz_ref[r0:r0 + A_CHUNK, c0:c0 + A_GW].astype(jnp.float32) * s)


def _gmlp_branch(vn, uz, ws, bias):
    n = vn.shape[0]
    rows = _tile(n, 512)
    row_spec = pl.BlockSpec((rows, BR_W), lambda i: (i, 0))
    return pl.pallas_call(
        _gmlp_kernel,
        grid=(n // rows,),
        in_specs=[row_spec, row_spec, _const_spec(ws), _const_spec(bias)],
        out_specs=row_spec,
        out_shape=jax.ShapeDtypeStruct((n, BR_W), jnp.bfloat16),
        compiler_params=_params("parallel"),
        name="gmlp_mix",
    )(vn, uz, ws, bias)


N_PAIR = C_HEADS // 2
PAIR_W = 2 * C_HD


def _scan_kernel(lw0_ref, lw1_ref, a0_ref, a1_ref, kd0_ref, kd1_ref, kk0_ref, kk1_ref, v0_ref, v1_ref,
                 r0_ref, r1_ref, s0_ref, y0_ref, y1_ref, s_ref, *, chunk):
    @pl.when(pl.program_id(1) == 0)
    def _():
        s_ref[...] = s0_ref[...]

    ck = chunk
    n_sub = lw0_ref.shape[0] // ck
    n2 = 2 * ck
    row = lax.broadcasted_iota(jnp.int32, (n2, n2), 0)
    col = lax.broadcasted_iota(jnp.int32, (n2, n2), 1)
    gap = row % ck - col % ck
    strict = (gap > 0, gap < 0)
    incl = (gap >= 0, gap <= 0)
    eye = (row == col).astype(jnp.float32)
    gap1 = lax.broadcasted_iota(jnp.int32, (ck, ck), 0) - lax.broadcasted_iota(jnp.int32, (ck, ck), 1)
    incl1 = (_bf((gap1 >= 0).astype(jnp.float32)), _bf((gap1 <= 0).astype(jnp.float32)))
    head0 = lax.broadcasted_iota(jnp.int32, (1, PAIR_W), 1) < C_HD

    def stack(x):
        return jnp.concatenate([jnp.where(head0, x, 0.0), jnp.where(head0, 0.0, x)], axis=0)

    def scaled(d, rows, lw_ref, a_ref, kd_ref, kk_ref, v_ref, r_ref):
        lw = lw_ref[rows, :]
        lw_hi, lw_lo = _split(lw)
        cum = (jnp.dot(incl1[d], lw_hi, preferred_element_type=jnp.float32)
               + jnp.dot(incl1[d], lw_lo, preferred_element_type=jnp.float32))
        p_in = jnp.exp(cum)
        p_inv = jnp.exp(-cum)
        kk = kk_ref[rows, :]
        return dict(at=-kk * jnp.exp(cum - lw), bt=kk * a_ref[rows, :] * p_inv, kt=kd_ref[rows, :] * p_inv,
                    rt=r_ref[rows, :] * p_in, v=v_ref[rows, :], p_tot=jnp.exp(jnp.sum(lw, axis=0, keepdims=True)))

    in_refs = ((lw0_ref, a0_ref, kd0_ref, kk0_ref, v0_ref, r0_ref), (lw1_ref, a1_ref, kd1_ref, kk1_ref, v1_ref, r1_ref))
    y_refs = (y0_ref, y1_ref)
    local = lambda d, t: t if d == 0 else n_sub - 1 - t
    rows_of = lambda c: slice(c * ck, (c + 1) * ck)
    q = {(d, t): scaled(d, rows_of(local(d, t)), *in_refs[d]) for t in range(n_sub) for d in range(N_DIR)}

    chains = [(d, p, t) for t in range(n_sub) for p in range(N_PAIR) for d in range(N_DIR)]
    ids = range(len(chains))
    sl = [slice(p * PAIR_W, (p + 1) * PAIR_W) for _, p, _ in chains]
    atx, btx, ktx, rtx, vx = ([stack(q[d, t][name][:, sl[i]]) for i, (d, _, t) in enumerate(chains)]
                              for name in ('at', 'bt', 'kt', 'rt', 'v'))
    big = [_dot_nt(jnp.concatenate([atx[i], rtx[i]], axis=0), jnp.concatenate([btx[i], ktx[i]], axis=0))
           for i in ids]
    a_ab = [jnp.where(strict[chains[i][0]], big[i][:n2, :n2], 0.0) for i in ids]
    a_ak = [jnp.where(strict[chains[i][0]], big[i][:n2, n2:], 0.0) for i in ids]
    a_rb = [jnp.where(incl[chains[i][0]], big[i][n2:, :n2], 0.0) for i in ids]
    a_rk = [jnp.where(incl[chains[i][0]], big[i][n2:, n2:], 0.0) for i in ids]

    tinv = [eye + a_ab[i] for i in ids]
    apow = [_dot(a_ab[i], a_ab[i]) for i in ids]
    span = 2
    while span < ck:
        if 2 * span < ck:
            both = [_dot(apow[i], jnp.concatenate([apow[i], tinv[i]], axis=1)) for i in ids]
            apow = [both[i][:, :n2] for i in ids]
            tinv = [tinv[i] + both[i][:, n2:] for i in ids]
        else:
            tinv = [tinv[i] + _dot(apow[i], tinv[i]) for i in ids]
        span *= 2
    resid = [(eye - tinv[i]) + _dot3_wide(a_ab[i], tinv[i]) for i in ids]
    tinv = [tinv[i] + _dot(tinv[i], resid[i]) for i in ids]

    av = [_dot(jnp.concatenate([a_ak[i], a_rk[i]], axis=0), vx[i]) for i in ids]
    wu = [_dot(tinv[i], jnp.concatenate([atx[i], av[i][:n2]], axis=1)) for i in ids]
    vxt = [jnp.transpose(vx[i]) for i in ids]
    bk = [jnp.concatenate([btx[i], ktx[i]], axis=0) for i in ids]

    s = {(d, p): s_ref[d, p] for d in range(N_DIR) for p in range(N_PAIR)}
    for t in range(n_sub):
        now = [i for i in ids if chains[i][2] == t]
        st = {i: jnp.transpose(s[chains[i][:2]]) for i in now}
        u = {i: _dot(wu[i][:, :PAIR_W], st[i]) + wu[i][:, PAIR_W:] for i in now}
        yx = {i: _dot(jnp.concatenate([rtx[i], a_rb[i]], axis=1), jnp.concatenate([st[i], u[i]], axis=0))
              + av[i][n2:] for i in now}
        ds = {i: _dot(jnp.concatenate([jnp.transpose(u[i]), vxt[i]], axis=1), bk[i]) for i in now}
        for i in now:
            d, p, _ = chains[i]
            s[d, p] = (s[d, p] + ds[i]) * q[d, t]['p_tot'][:, sl[i]]
            y_refs[d][rows_of(local(d, t)), sl[i]] = yx[i][:ck] + yx[i][ck:]
    for (d, p), val in s.items():
        s_ref[d, p] = val


def _rwkv_scan(lw, a, kd, kk, v, r, s0):
    _, b_, t_, w_ = lw.shape
    ck = _tile(t_, SCAN_CHUNK)
    rows = ck * math.gcd(t_ // ck, SCAN_CHUNKS_PER_STEP)
    nb = t_ // rows
    fwd = pl.BlockSpec((None, None, rows, w_), lambda b, c: (0, b, c, 0))
    bwd = pl.BlockSpec((None, None, rows, w_), lambda b, c: (1, b, nb - 1 - c, 0))
    fwd_shared = pl.BlockSpec((None, rows, w_), lambda b, c: (b, c, 0))
    bwd_shared = pl.BlockSpec((None, rows, w_), lambda b, c: (b, nb - 1 - c, 0))
    state_spec = pl.BlockSpec((N_DIR, None, N_PAIR, PAIR_W, PAIR_W), lambda b, c: (0, b, 0, 0, 0))
    y0, y1, s_fin = pl.pallas_call(
        functools.partial(_scan_kernel, chunk=ck),
        grid=(b_, nb),
        in_specs=[fwd, bwd, fwd, bwd, fwd, bwd, fwd_shared, bwd_shared, fwd_shared, bwd_shared,
                  fwd_shared, bwd_shared, state_spec],
        out_specs=[fwd_shared, bwd_shared, state_spec],
        out_shape=[jax.ShapeDtypeStruct((b_, t_, w_), jnp.float32), jax.ShapeDtypeStruct((b_, t_, w_), jnp.float32),
                   jax.ShapeDtypeStruct((N_DIR, b_, N_PAIR, PAIR_W, PAIR_W), jnp.float32)],
        compiler_params=_params("parallel", "arbitrary"),
        name="rwkv_scan",
    )(lw, lw, a, a, kd, kd, kk, kk, v, v, r, r, s0)
    return y0, y1, s_fin


def _rope_tables(n_tok):
    rows = n_tok // GRID_W
    row = jnp.broadcast_to(jnp.arange(rows)[:, None], (rows, GRID_W)).reshape(-1)
    col = jnp.broadcast_to(jnp.arange(GRID_W)[None, :], (rows, GRID_W)).reshape(-1)
    nf = B_HD // 4
    inv = ROPE_BASE ** (-jnp.arange(nf, dtype=jnp.float32) / nf)
    ang_r = row.astype(jnp.float32)[:, None] * inv[None, :]
    ang_c = col.astype(jnp.float32)[:, None] * inv[None, :]
    cos = jnp.concatenate([jnp.cos(ang_r)] * 2 + [jnp.cos(ang_c)] * 2, axis=1)
    sin = jnp.concatenate([-jnp.sin(ang_r), jnp.sin(ang_r), -jnp.sin(ang_c), jnp.sin(ang_c)], axis=1)
    return jnp.tile(cos, (1, 2)), jnp.tile(sin, (1, 2))


def _layer_tables(p, lam_init):
    w = _bf(p['w_in'])
    off, cols = 0, {}
    for name, size in zip(('dk', 'dv', 'kr', 'vr', 'wl', 'al', 'dq', 'r', 'u', 'va', 'za', 'zb', 'zc', 'gl'),
                          STATE_SIZES + OUT_SIZES):
        cols[name] = (off, off + size)
        off += size
    sl = lambda a, b=None: w[:, cols[a][0]:cols[b or a][1]]
    grp = jnp.arange(BR_W) // C_HD
    ones64 = _bf((grp[:, None] == grp[None, :]).astype(jnp.float32))
    lora_rows = jnp.arange(N_DIR * W_LORA) // W_LORA

    def padded(w2):
        stacked = jnp.concatenate([w2, w2], axis=1)
        return _bf(jnp.where((lora_rows[None, :, None] == jnp.arange(N_DIR)[:, None, None]), stacked, 0.0))

    row = lambda v: v.reshape(1, -1).astype(jnp.float32)
    return dict(
        wk=sl('dk'), wv=sl('dv'), wq=sl('dq'), ones64=ones64, mean64=_bf(ones64.astype(jnp.float32) / C_HD),
        gk=row(jnp.tile(p['d_knorm'], BR_W // B_HD)), gq=row(jnp.tile(p['d_qnorm'], BR_W // B_HD)),
        wkvr=jnp.concatenate([sl('kr'), sl('vr'), sl('r')], axis=1), wla=sl('wl', 'al'),
        taps=jnp.concatenate([p['r_conv'][1], p['r_conv'][2], p['r_conv'][0]], axis=1).reshape(3, 1, 3 * BR_W),
        r_kk=row(p['r_kk']), r_ka=row(p['r_ka']), r_rk=row(p['r_rk']),
        w2=padded(p['r_w2']), a2=padded(p['r_a2']),
        w0=p['r_w0'].reshape(N_DIR, 1, BR_W), a0=p['r_a0'].reshape(N_DIR, 1, BR_W),
        wgates=sl('u', 'zc'), a_ln_g=row(p['a_ln_g']), a_ln_b=row(p['a_ln_b']),
        wgl=sl('gl'), wbr=_bf(p['w_br']), wout=_bf(p['w_out']),
        r_ln_g=row(p['r_ln_g']), r_ln_b=row(p['r_ln_b']),
        subln_g=row(p['d_subln_g'] * (1.0 - lam_init)),
        a_ws=_bf(p['a_ws']),
        a_bias=jnp.repeat(jnp.swapaxes(p['a_bs'], 0, 1), A_GW, axis=1).astype(jnp.float32),
    )


def _stream(x, gs, sh, gate, lw, p, rope, lam, ctx_kv, s0, need_out):
    b_, t_, d = x.shape
    n = b_ * t_
    x2 = x.reshape(n, d)
    k_att, v_att, q = _qkv_proj(x2, t_, gs, sh, lw, rope)
    kk, v, r, kd, lwd, a, bonus = _rwkv_in_proj(x2, t_, gs, sh, lw)
    as3 = lambda z: z.reshape(b_, t_, BR_W)
    as4 = lambda z: z.reshape(N_DIR, b_, t_, BR_W)
    k_att, v_att = as3(k_att), as3(v_att)
    y0, y1, s_fin = _rwkv_scan(as4(lwd), as4(a), as4(kd), as3(kk), as3(v), as3(r), s0)
    if not need_out:
        return None, (k_att, v_att), s_fin
    uz, vn, szb, szc = _gates_proj(x2, t_, gs, sh, lw)
    ya = _gmlp_branch(vn, uz, lw['a_ws'], lw['a_bias'])
    ks = [k_att] + ([ctx_kv[0]] if ctx_kv is not None else [])
    vs = [v_att] + ([ctx_kv[1]] if ctx_kv is not None else [])
    yb = _diff_attention(as3(q), as3(szb), lw['subln_g'], ks, vs, lam).reshape(n, BR_W)
    out = _merge(x2, t_, gs, sh, gate, ya, yb, y0.reshape(n, BR_W), y1.reshape(n, BR_W), bonus, szc, lw)
    return out.reshape(b_, t_, d), (k_att, v_att), s_fin


def _layer(x, xc, c_act, cc_act, rope, lam_init, p, update_ctx):
    d = D_MODEL
    b_ = x.shape[0]
    lw = _layer_tables(p, lam_init)
    mod_all = _mod_rows(jnp.concatenate([c_act, cc_act[None]], axis=0), p['w_mod'], p['b_mod'])
    mod, mod_c = mod_all[:b_], mod_all[b_:]
    rows3 = lambda z: z[:, None, :]
    g = p['norm_g']
    lp = p['d_lam']
    lam = jnp.exp(jnp.sum(lp[0] * lp[1])) - jnp.exp(jnp.sum(lp[2] * lp[3])) + lam_init
    s_zero = jnp.zeros((N_DIR, b_, N_PAIR, PAIR_W, PAIR_W), jnp.float32)

    xc_next, ctx_kv, s_ctx = _stream(xc, rows3(g * (1.0 + mod_c[:, d:2 * d])), rows3(mod_c[:, :d]),
                                     rows3(mod_c[:, 2 * d:]), lw, p, None, lam, None, s_zero, update_ctx)
    x_next, _, _ = _stream(x, rows3(g * (1.0 + mod[:, d:2 * d])), rows3(mod[:, :d]), rows3(mod[:, 2 * d:]),
                           lw, p, rope, lam, ctx_kv, s_ctx, True)
    return x_next, xc_next


def kernel(x, c, ctx, c_ctx, w_mod, b_mod, norm_g, w_in, a_ln_g, a_ln_b, a_ws, a_bs, d_qnorm, d_knorm, d_lam,
           d_subln_g, r_conv, r_w0, r_w2, r_a0, r_a2, r_kk, r_ka, r_rk, r_ln_g, r_ln_b, w_br, w_out):
    rope = _rope_tables(x.shape[1])
    c_act = jax.nn.silu(c)
    cc_act = jax.nn.silu(c_ctx)
    xc = ctx
    depth = w_in.shape[0]
    for l in range(depth):
        p = dict(w_mod=w_mod[l], b_mod=b_mod[l], norm_g=norm_g[l], w_in=w_in[l],
                 a_ln_g=a_ln_g[l], a_ln_b=a_ln_b[l], a_ws=a_ws[l], a_bs=a_bs[l],
                 d_qnorm=d_qnorm[l], d_knorm=d_knorm[l], d_lam=d_lam[l], d_subln_g=d_subln_g[l],
                 r_conv=r_conv[l], r_w0=r_w0[l], r_w2=r_w2[l], r_a0=r_a0[l], r_a2=r_a2[l],
                 r_kk=r_kk[l], r_ka=r_ka[l], r_rk=r_rk[l], r_ln_g=r_ln_g[l], r_ln_b=r_ln_b[l],
                 w_br=w_br[l], w_out=w_out[l])
        lam_init = 0.8 - 0.6 * math.exp(-0.3 * l)
        x, xc = _layer(x, xc, c_act, cc_act, rope, lam_init, p, l < depth - 1)
    return x
```

```python
import functools
import math

import jax
import jax.numpy as jnp
from jax import lax
from jax.experimental import pallas as pl
from jax.experimental.pallas import tpu as pltpu

D_MODEL = 1024
GRID_W = 64
N_BRANCH = 3
BR_W = D_MODEL // 2
A_GROUPS = 4
A_GW = BR_W // A_GROUPS
A_CHUNK = 128
B_HD = 64
B_VD = 2 * B_HD
B_HEADS = BR_W // B_VD
ROPE_BASE = 10000.0
C_HD = 64
C_HEADS = BR_W // C_HD
N_DIR = 2
W_LORA = 64
A_LORA = 64
NORM_EPS = 1e-6
LN_EPS = 1e-5
GN_EPS = 64e-5
STATE_SIZES = (BR_W, BR_W, BR_W, BR_W, N_DIR * W_LORA, N_DIR * A_LORA)
OUT_SIZES = (BR_W,) * 7 + (N_BRANCH * D_MODEL,)
STATE_COLS = sum(STATE_SIZES)
OUT_COLS = sum(OUT_SIZES)

ROW_TILE = 512
HALO_ROWS = 16
SCAN_CHUNK = 64
SCAN_CHUNKS_PER_STEP = 2
ATTN_TQ = 512
ATTN_ROWS = 256
ATTN_UNROLL = 8
ATTN_SAFE_BOUND = 60.0
VMEM_LIMIT_BYTES = 48 * 1024 * 1024
LOG2E = 1.4426950408889634
DECAY_SCALE = math.exp(-0.5)


def _tile(n, pref):
    if n <= pref:
        return n
    t = pref
    while n % t:
        t //= 2
    return t


def _bf(x):
    return x.astype(jnp.bfloat16)


def _dot(a, b):
    return jnp.dot(_bf(a), _bf(b), preferred_element_type=jnp.float32)


def _dot_nt(a, b):
    return lax.dot_general(_bf(a), _bf(b), (((1,), (1,)), ((), ())), preferred_element_type=jnp.float32)


def _split(x):
    hi = _bf(x)
    return hi, _bf(x - hi.astype(jnp.float32))


def _dot3_wide(a, b):
    ah, al = _split(a)
    bh, bl = _split(b)
    n = b.shape[1]
    f = lambda x, y: jnp.dot(x, y, preferred_element_type=jnp.float32)
    wide = f(ah, jnp.concatenate([bh, bl], axis=1))
    return wide[:, :n] + (wide[:, n:] + f(al, bh))


def _group_sum(x, ones_ref, exact=False):
    g = ones_ref[...]
    f = lambda a: jnp.dot(a, g, preferred_element_type=jnp.float32)
    blocks = []
    for c0 in range(0, x.shape[1], PAIR_W):
        xb = x[:, c0:c0 + PAIR_W]
        if exact:
            hi, lo = _split(xb)
            blocks.append(f(hi) + f(lo))
        else:
            blocks.append(f(_bf(xb)))
    return jnp.concatenate(blocks, axis=1)


def _params(*semantics):
    return pltpu.CompilerParams(dimension_semantics=semantics, vmem_limit_bytes=VMEM_LIMIT_BYTES)


def _mod_kernel(a_ref, w_ref, b_ref, o_ref):
    o_ref[...] = jnp.dot(a_ref[...], w_ref[...], preferred_element_type=jnp.float32) + b_ref[...]


def _mod_rows(act, w_mod, b_mod):
    r = act.shape[0]
    a = _bf(jnp.pad(act, ((0, -r % 16), (0, 0))))
    w = _bf(w_mod)
    b = b_mod.reshape(1, -1)
    out = pl.pallas_call(
        _mod_kernel,
        grid=(1,),
        in_specs=[_const_spec(a), _const_spec(w), _const_spec(b)],
        out_specs=pl.BlockSpec((a.shape[0], w.shape[1]), lambda i: (0, 0)),
        out_shape=jax.ShapeDtypeStruct((a.shape[0], w.shape[1]), jnp.float32),
        compiler_params=_params("arbitrary"), name="mod_rows",
    )(a, w, b)
    return out[:r]


def _modulated(x, gs, sh):
    rinv = lax.rsqrt(jnp.mean(x * x, axis=-1, keepdims=True) + NORM_EPS)
    return _bf(x * rinv * gs + sh)


def _qkv_kernel(x_ref, gs_ref, sh_ref, wk_ref, wv_ref, wq_ref, ones_ref, gk_ref, gq_ref, *rest, rope):
    if rope:
        cos_ref, sin_ref, k_out, v_out, q_out = rest
    else:
        k_out, v_out, q_out = rest
    h = _modulated(x_ref[...], gs_ref[...], sh_ref[...])

    if rope:
        cos = jnp.concatenate([cos_ref[...]] * B_HEADS, axis=1)
        sin = jnp.concatenate([sin_ref[...]] * B_HEADS, axis=1)
        lane = lax.broadcasted_iota(jnp.int32, (1, BR_W), 1)
        first = (lane % (B_HD // 2)) < (B_HD // 4)

    def normed(z, g_ref):
        ss = _group_sum(z * z, ones_ref)
        y = z * lax.rsqrt(ss * (1.0 / B_HD) + NORM_EPS) * g_ref[...]
        if rope:
            partner = jnp.where(first, pltpu.roll(y, BR_W - B_HD // 4, 1), pltpu.roll(y, B_HD // 4, 1))
            y = y * cos + partner * sin
        return y

    zk = jnp.dot(h, wk_ref[...], preferred_element_type=jnp.float32)
    zq = jnp.dot(h, wq_ref[...], preferred_element_type=jnp.float32)
    v_out[...] = _bf(jnp.dot(h, wv_ref[...], preferred_element_type=jnp.float32))
    k_out[...] = _bf(normed(zk, gk_ref))
    q_out[...] = _bf(normed(zq, gq_ref) * (B_HD ** -0.5 * LOG2E))


def _rwkv_in_kernel(x_ref, xp_ref, xn_ref, gs_ref, sh_ref, w_ref, wla_ref, w2_ref, a2_ref, w0_ref, a0_ref,
                    taps_ref, rkk_ref, rka_ref, rrk_ref, ones_ref,
                    kk_out, v_out, r_out, kd_out, lw_out, a_out, bonus_out, *, per_batch):
    i = pl.program_id(0)
    gs, sh = gs_ref[...], sh_ref[...]
    h = _modulated(x_ref[...], gs, sh)
    w = w_ref[...]
    zp = jnp.dot(_modulated(xp_ref[...], gs, sh), w, preferred_element_type=jnp.float32)[HALO_ROWS - 1:HALO_ROWS]
    zn = jnp.dot(_modulated(xn_ref[...], gs, sh), w, preferred_element_type=jnp.float32)[0:1]
    zp = jnp.where(i % per_batch == 0, 0.0, zp)
    zn = jnp.where(i % per_batch == per_batch - 1, 0.0, zn)
    la = jnp.dot(h, wla_ref[...], preferred_element_type=jnp.float32)
    n_l = N_DIR * W_LORA
    wl = _bf(jnp.tanh(la[:, :n_l]))
    al = _bf(la[:, n_l:])
    w_log = [w0_ref[d] + jnp.dot(wl, w2_ref[d], preferred_element_type=jnp.float32) for d in range(N_DIR)]
    a_log = [a0_ref[d] + jnp.dot(al, a2_ref[d], preferred_element_type=jnp.float32) for d in range(N_DIR)]
    z = jnp.dot(h, w, preferred_element_type=jnp.float32)
    tm = z.shape[0]
    rowi = lax.broadcasted_iota(jnp.int32, (tm, 1), 0)
    before = jnp.where(rowi == 0, zp, pltpu.roll(z, 1, 0))
    after = jnp.where(rowi == tm - 1, zn, pltpu.roll(z, tm - 1, 0))
    z = before * taps_ref[0] + z * taps_ref[1] + after * taps_ref[2]
    k, v, r = z[:, :BR_W], z[:, BR_W:2 * BR_W], z[:, 2 * BR_W:]
    v_out[...] = v
    r_out[...] = r
    kk = k * rkk_ref[...]
    kk_out[...] = kk / jnp.maximum(jnp.sqrt(_group_sum(kk * kk, ones_ref)), 1e-12)
    bonus_out[...] = _group_sum(r * k * rrk_ref[...], ones_ref) * v

    for d in range(N_DIR):
        lw_out[d] = -DECAY_SCALE * jax.nn.sigmoid(w_log[d])
        a = jax.nn.sigmoid(a_log[d])
        a_out[d] = a
        kd_out[d] = k * (1.0 + (a - 1.0) * rka_ref[...])


def _gates_kernel(x_ref, gs_ref, sh_ref, w_ref, lng_ref, lnb_ref, uz_out, vn_out, szb_out, szc_out):
    h = _modulated(x_ref[...], gs_ref[...], sh_ref[...])
    z = jnp.dot(h, w_ref[...], preferred_element_type=jnp.float32)
    u, va, za, zb, zc = (z[:, i * BR_W:(i + 1) * BR_W] for i in range(5))
    uz_out[...] = _bf(jax.nn.gelu(u) * jax.nn.silu(za))
    va = jax.nn.gelu(va)
    mu = jnp.mean(va, axis=-1, keepdims=True)
    var = jnp.mean(jnp.square(va - mu), axis=-1, keepdims=True)
    vn_out[...] = _bf((va - mu) * lax.rsqrt(var + LN_EPS) * lng_ref[...] + lnb_ref[...])
    szb_out[...] = _bf(jax.nn.silu(zb))
    szc_out[...] = _bf(jax.nn.silu(zc))


def _row_specs(n, t_, n_mod):
    tm = _tile(t_, ROW_TILE)
    per_batch = t_ // tm
    x_spec = pl.BlockSpec((tm, D_MODEL), lambda i: (i, 0))
    mod_spec = pl.BlockSpec((None, 1, D_MODEL), (lambda i: (i // per_batch, 0, 0)) if n_mod > 1 else (lambda i: (0, 0, 0)))
    return tm, per_batch, x_spec, mod_spec


def _const_spec(a):
    nd = a.ndim
    return pl.BlockSpec(a.shape, lambda i: (0,) * nd)


def _out_spec(tm, w):
    return pl.BlockSpec((tm, w), lambda i: (i, 0))


def _qkv_proj(x2, t_, gs, sh, lw, rope):
    n = x2.shape[0]
    tm, per_batch, x_spec, mod_spec = _row_specs(n, t_, gs.shape[0])
    consts = [lw['wk'], lw['wv'], lw['wq'], lw['ones64'], lw['gk'], lw['gq']]
    in_specs = [x_spec, mod_spec, mod_spec] + [_const_spec(a) for a in consts]
    args = [x2, gs, sh] + consts
    if rope is not None:
        rope_spec = pl.BlockSpec((tm, B_VD), lambda i: (i % per_batch, 0))
        in_specs += [rope_spec, rope_spec]
        args += list(rope)
    out = jax.ShapeDtypeStruct((n, BR_W), jnp.bfloat16)
    return pl.pallas_call(
        functools.partial(_qkv_kernel, rope=rope is not None),
        grid=(n // tm,), in_specs=in_specs, out_specs=[_out_spec(tm, BR_W)] * 3, out_shape=[out] * 3,
        compiler_params=_params("parallel"), name="qkv_proj",
    )(*args)


def _rwkv_in_proj(x2, t_, gs, sh, lw):
    n = x2.shape[0]
    tm, per_batch, x_spec, mod_spec = _row_specs(n, t_, gs.shape[0])
    halo_per_tile = tm // HALO_ROWS
    prev_spec = pl.BlockSpec((HALO_ROWS, D_MODEL), lambda i: (jnp.maximum(i * halo_per_tile - 1, 0), 0))
    next_spec = pl.BlockSpec((HALO_ROWS, D_MODEL),
                             lambda i: (jnp.minimum((i + 1) * halo_per_tile, n // HALO_ROWS - 1), 0))
    consts = [lw['wkvr'], lw['wla'], lw['w2'], lw['a2'], lw['w0'], lw['a0'], lw['taps'], lw['r_kk'], lw['r_ka'],
              lw['r_rk'], lw['ones64']]
    f32 = jax.ShapeDtypeStruct((n, BR_W), jnp.float32)
    f32d = jax.ShapeDtypeStruct((N_DIR, n, BR_W), jnp.float32)
    row_spec = _out_spec(tm, BR_W)
    dir_spec = pl.BlockSpec((N_DIR, tm, BR_W), lambda i: (0, i, 0))
    return pl.pallas_call(
        functools.partial(_rwkv_in_kernel, per_batch=per_batch),
        grid=(n // tm,),
        in_specs=[x_spec, prev_spec, next_spec, mod_spec, mod_spec] + [_const_spec(a) for a in consts],
        out_specs=[row_spec] * 3 + [dir_spec] * 3 + [row_spec],
        out_shape=[f32] * 3 + [f32d] * 3 + [f32],
        compiler_params=_params("parallel"), name="rwkv_in_proj",
    )(x2, x2, x2, gs, sh, *consts)


def _gates_proj(x2, t_, gs, sh, lw):
    n = x2.shape[0]
    tm, _, x_spec, mod_spec = _row_specs(n, t_, gs.shape[0])
    consts = [lw['wgates'], lw['a_ln_g'], lw['a_ln_b']]
    out = jax.ShapeDtypeStruct((n, BR_W), jnp.bfloat16)
    return pl.pallas_call(
        _gates_kernel,
        grid=(n // tm,), in_specs=[x_spec, mod_spec, mod_spec] + [_const_spec(a) for a in consts],
        out_specs=[_out_spec(tm, BR_W)] * 4, out_shape=[out] * 4,
        compiler_params=_params("parallel"), name="gates_proj",
    )(x2, gs, sh, *consts)


def _merge_kernel(x_ref, gs_ref, sh_ref, gate_ref, ya_ref, yb_ref, y0_ref, y1_ref, bonus_ref, szc_ref,
                  mean_ref, lng_ref, lnb_ref, wgl_ref, wbr_ref, wout_ref, o_ref):
    x = x_ref[...]
    y = y0_ref[...] + y1_ref[...]
    dev = y - _group_sum(y, mean_ref, exact=True)
    var = _group_sum(dev * dev, mean_ref)
    yc = (dev * lax.rsqrt(var + GN_EPS) * lng_ref[...] + lnb_ref[...] + bonus_ref[...]) * szc_ref[...].astype(jnp.float32)
    h = _modulated(x, gs_ref[...], sh_ref[...])
    g = jax.nn.sigmoid(jnp.dot(h, wgl_ref[...], preferred_element_type=jnp.float32))
    ys = (ya_ref[...], yb_ref[...], _bf(yc))
    mix = None
    for i in range(N_BRANCH):
        up = jnp.dot(ys[i], wbr_ref[i], preferred_element_type=jnp.float32)
        term = g[:, i * D_MODEL:(i + 1) * D_MODEL] * up
        mix = term if mix is None else mix + term
    o_ref[...] = x + gate_ref[...] * jnp.dot(_bf(mix), wout_ref[...], preferred_element_type=jnp.float32)


def _merge(x2, t_, gs, sh, gate, ya, yb, y0, y1, bonus, szc, lw):
    n = x2.shape[0]
    tm, _, x_spec, mod_spec = _row_specs(n, t_, gs.shape[0])
    consts = [lw['mean64'], lw['r_ln_g'], lw['r_ln_b'], lw['wgl'], lw['wbr'], lw['wout']]
    br = _out_spec(tm, BR_W)
    return pl.pallas_call(
        _merge_kernel,
        grid=(n // tm,),
        in_specs=[x_spec, mod_spec, mod_spec, mod_spec] + [br] * 6 + [_const_spec(a) for a in consts],
        out_specs=_out_spec(tm, D_MODEL), out_shape=jax.ShapeDtypeStruct((n, D_MODEL), jnp.float32),
        compiler_params=_params("parallel"), name="merge_out",
    )(x2, gs, sh, gate, ya, yb, y0, y1, bonus, szc, *consts)


def _attn_kernel(lam_ref, q_ref, szb_ref, g_ref, *refs, chunks):
    n_src = len(chunks)
    k_refs = refs[:n_src]
    v_refs = refs[n_src:2 * n_src]
    o_ref = refs[2 * n_src]
    kmax_scr = refs[2 * n_src + 1]
    q = q_ref[0]
    tq = q.shape[0]
    rows = min(tq, ATTN_ROWS)
    lane = lax.broadcasted_iota(jnp.int32, (1, B_VD), 1)
    sub = (lane < B_HD, lane >= B_HD)
    zero = jnp.zeros_like(q)
    q_sub = [jnp.where(mask, q, zero) for mask in sub]
    qs = [qj[r:r + rows] for r in range(0, tq, rows) for qj in q_sub]

    @pl.when(pl.program_id(2) == 0)
    def _():
        for j, mask in enumerate(sub):
            best = jnp.zeros((1, 1), jnp.float32)
            for k_ref in k_refs:
                kf = k_ref[0].astype(jnp.float32)
                norm2 = jnp.sum(jnp.where(mask, kf * kf, 0.0), axis=-1, keepdims=True)
                best = jnp.maximum(best, jnp.max(norm2, axis=0, keepdims=True))
            kmax_scr[j] = jnp.broadcast_to(best, kmax_scr.shape[1:])

    qf = q.astype(jnp.float32)
    bound_sub = [jnp.sqrt(jnp.sum(jnp.where(mask, qf * qf, 0.0), axis=-1, keepdims=True) * kmax_scr[j][0:1, 0:1])
                 for j, mask in enumerate(sub)]
    bounds = [bj[r:r + rows] for r in range(0, tq, rows) for bj in bound_sub]
    worst = jnp.max(jnp.maximum(bound_sub[0], bound_sub[1]))

    def scores(k):
        return [lax.dot_general(qj, k, (((1,), (1,)), ((), ())), preferred_element_type=jnp.float32) for qj in qs]

    def fixed_update(carry, k, v):
        ps = [jnp.exp2(s - b) for s, b in zip(scores(k), bounds)]
        pv = [jnp.dot(p.astype(v.dtype), v, preferred_element_type=jnp.float32) for p in ps]
        return tuple((l + jnp.sum(p, axis=-1, keepdims=True), acc + o) for (l, acc), p, o in zip(carry, ps, pv))

    def online_update(carry, k, v):
        ss = scores(k)
        m_new = [jnp.maximum(m, jnp.max(s, axis=-1, keepdims=True)) for s, (m, _, _) in zip(ss, carry)]
        ps = [jnp.exp2(s - mn) for s, mn in zip(ss, m_new)]
        pv = [jnp.dot(p.astype(v.dtype), v, preferred_element_type=jnp.float32) for p in ps]
        new = []
        for j, (m, l, acc) in enumerate(carry):
            alpha = jnp.exp2(m - m_new[j])
            new.append((m_new[j], alpha * l + jnp.sum(ps[j], axis=-1, keepdims=True), alpha * acc + pv[j]))
        return tuple(new)

    def over_keys(update, carry):
        for k_ref, v_ref, (n_chunk, tk) in zip(k_refs, v_refs, chunks):
            def body(i, c, k_ref=k_ref, v_ref=v_ref, tk=tk):
                off = pl.multiple_of(i * tk, tk)
                return update(c, k_ref[0, pl.ds(off, tk), :], v_ref[0, pl.ds(off, tk), :])
            carry = lax.fori_loop(0, n_chunk, body, carry, unroll=math.gcd(n_chunk, ATTN_UNROLL))
        return carry

    def finish(sums):
        lam = lam_ref[0]
        for b in range(tq // rows):
            (l1, acc1), (l2, acc2) = sums[2 * b], sums[2 * b + 1]
            att = acc1 / l1 - lam * (acc2 / l2)
            att = att * lax.rsqrt(jnp.mean(att * att, axis=-1, keepdims=True) + NORM_EPS) * g_ref[...]
            o_ref[0, b * rows:(b + 1) * rows, :] = _bf(att * szb_ref[0, b * rows:(b + 1) * rows, :].astype(jnp.float32))

    zeros = lambda w: jnp.zeros((rows, w), jnp.float32)

    @pl.when(worst <= ATTN_SAFE_BOUND)
    def _():
        finish(over_keys(fixed_update, tuple((zeros(1), zeros(B_VD)) for _ in qs)))

    @pl.when(worst > ATTN_SAFE_BOUND)
    def _():
        carry = over_keys(online_update, tuple((jnp.full((rows, 1), -jnp.inf, jnp.float32), zeros(1), zeros(B_VD))
                                               for _ in qs))
        finish([(l, acc) for _, l, acc in carry])


def _diff_attention(q, szb, subln_g, ks, vs, lam):
    b_, t_, _ = q.shape
    tq = _tile(t_, ATTN_TQ)
    chunks = tuple((k.shape[1] // _tile(k.shape[1], 512), _tile(k.shape[1], 512)) for k in ks)
    kv_specs = [pl.BlockSpec((1, k.shape[1], B_VD), lambda b, h, i: (b, 0, h)) for k in ks]
    q_spec = pl.BlockSpec((1, tq, B_VD), lambda b, h, i: (b, i, h))
    return pl.pallas_call(
        functools.partial(_attn_kernel, chunks=chunks),
        grid=(b_, B_HEADS, t_ // tq),
        in_specs=[pl.BlockSpec(memory_space=pltpu.SMEM), q_spec, q_spec,
                  pl.BlockSpec((1, B_VD), lambda b, h, i: (0, 0))] + kv_specs + kv_specs,
        out_specs=q_spec,
        out_shape=jax.ShapeDtypeStruct((b_, t_, BR_W), jnp.bfloat16),
        scratch_shapes=[pltpu.VMEM((2, 8, B_VD), jnp.float32)],
        compiler_params=_params("parallel", "parallel", "arbitrary"),
        name="diff_attention",
    )(lam.reshape(1).astype(jnp.float32), q, szb, subln_g, *ks, *vs)


def _gmlp_kernel(v_ref, uz_ref, ws_ref, bias_ref, o_ref):
    rows = v_ref.shape[0]
    for n in range(rows // A_CHUNK):
        r0 = n * A_CHUNK
        for g in range(A_GROUPS):
            c0 = g * A_GW
            s = jnp.dot(ws_ref[g], v_ref[r0:r0 + A_CHUNK, c0:c0 + A_GW], preferred_element_type=jnp.float32)
            s = s + bias_ref[:, c0:c0 + A_GW]
            o_ref[r0:r0 + A_CHUNK, c0:c0 + A_GW] = _bf(uz_ref[r0:r0 + A_CHUNK, c0:c0 + A_GW].astype(jnp.float32) * s)


def _gmlp_branch(vn, uz, ws, bias):
    n = vn.shape[0]
    rows = _tile(n, 512)
    row_spec = pl.BlockSpec((rows, BR_W), lambda i: (i, 0))
    return pl.pallas_call(
        _gmlp_kernel,
        grid=(n // rows,),
        in_specs=[row_spec, row_spec, _const_spec(ws), _const_spec(bias)],
        out_specs=row_spec,
        out_shape=jax.ShapeDtypeStruct((n, BR_W), jnp.bfloat16),
        compiler_params=_params("parallel"),
        name="gmlp_mix",
    )(vn, uz, ws, bias)


N_PAIR = C_HEADS // 2
PAIR_W = 2 * C_HD


def _scan_kernel(lw0_ref, lw1_ref, a0_ref, a1_ref, kd0_ref, kd1_ref, kk0_ref, kk1_ref, v0_ref, v1_ref,
                 r0_ref, r1_ref, s0_ref, y0_ref, y1_ref, s_ref, *, chunk):
    @pl.when(pl.program_id(1) == 0)
    def _():
        s_ref[...] = s0_ref[...]

    ck = chunk
    n_sub = lw0_ref.shape[0] // ck
    n2 = 2 * ck
    row = lax.broadcasted_iota(jnp.int32, (n2, n2), 0)
    col = lax.broadcasted_iota(jnp.int32, (n2, n2), 1)
    gap = row % ck - col % ck
    strict = (gap > 0, gap < 0)
    incl = (gap >= 0, gap <= 0)
    eye = (row == col).astype(jnp.float32)
    gap1 = lax.broadcasted_iota(jnp.int32, (ck, ck), 0) - lax.broadcasted_iota(jnp.int32, (ck, ck), 1)
    incl1 = (_bf((gap1 >= 0).astype(jnp.float32)), _bf((gap1 <= 0).astype(jnp.float32)))
    head0 = lax.broadcasted_iota(jnp.int32, (1, PAIR_W), 1) < C_HD

    def stack(x):
        return jnp.concatenate([jnp.where(head0, x, 0.0), jnp.where(head0, 0.0, x)], axis=0)

    def scaled(d, rows, lw_ref, a_ref, kd_ref, kk_ref, v_ref, r_ref):
        lw = lw_ref[rows, :]
        lw_hi, lw_lo = _split(lw)
        cum = (jnp.dot(incl1[d], lw_hi, preferred_element_type=jnp.float32)
               + jnp.dot(incl1[d], lw_lo, preferred_element_type=jnp.float32))
        p_in = jnp.exp(cum)
        p_inv = jnp.exp(-cum)
        kk = kk_ref[rows, :]
        return dict(at=-kk * jnp.exp(cum - lw), bt=kk * a_ref[rows, :] * p_inv, kt=kd_ref[rows, :] * p_inv,
                    rt=r_ref[rows, :] * p_in, v=v_ref[rows, :], p_tot=jnp.exp(jnp.sum(lw, axis=0, keepdims=True)))

    in_refs = ((lw0_ref, a0_ref, kd0_ref, kk0_ref, v0_ref, r0_ref), (lw1_ref, a1_ref, kd1_ref, kk1_ref, v1_ref, r1_ref))
    y_refs = (y0_ref, y1_ref)
    local = lambda d, t: t if d == 0 else n_sub - 1 - t
    rows_of = lambda c: slice(c * ck, (c + 1) * ck)
    q = {(d, t): scaled(d, rows_of(local(d, t)), *in_refs[d]) for t in range(n_sub) for d in range(N_DIR)}

    chains = [(d, p, t) for t in range(n_sub) for p in range(N_PAIR) for d in range(N_DIR)]
    ids = range(len(chains))
    sl = [slice(p * PAIR_W, (p + 1) * PAIR_W) for _, p, _ in chains]
    atx, btx, ktx, rtx, vx = ([stack(q[d, t][name][:, sl[i]]) for i, (d, _, t) in enumerate(chains)]
                              for name in ('at', 'bt', 'kt', 'rt', 'v'))
    big = [_dot_nt(jnp.concatenate([atx[i], rtx[i]], axis=0), jnp.concatenate([btx[i], ktx[i]], axis=0))
           for i in ids]
    a_ab = [jnp.where(strict[chains[i][0]], big[i][:n2, :n2], 0.0) for i in ids]
    a_ak = [jnp.where(strict[chains[i][0]], big[i][:n2, n2:], 0.0) for i in ids]
    a_rb = [jnp.where(incl[chains[i][0]], big[i][n2:, :n2], 0.0) for i in ids]
    a_rk = [jnp.where(incl[chains[i][0]], big[i][n2:, n2:], 0.0) for i in ids]

    tinv = [eye + a_ab[i] for i in ids]
    apow = [_dot(a_ab[i], a_ab[i]) for i in ids]
    span = 2
    while span < ck:
        if 2 * span < ck:
            both = [_dot(apow[i], jnp.concatenate([apow[i], tinv[i]], axis=1)) for i in ids]
            apow = [both[i][:, :n2] for i in ids]
            tinv = [tinv[i] + both[i][:, n2:] for i in ids]
        else:
            tinv = [tinv[i] + _dot(apow[i], tinv[i]) for i in ids]
        span *= 2
    resid = [(eye - tinv[i]) + _dot3_wide(a_ab[i], tinv[i]) for i in ids]
    tinv = [tinv[i] + _dot(tinv[i], resid[i]) for i in ids]

    av = [_dot(jnp.concatenate([a_ak[i], a_rk[i]], axis=0), vx[i]) for i in ids]
    wu = [_dot(tinv[i], jnp.concatenate([atx[i], av[i][:n2]], axis=1)) for i in ids]
    vxt = [jnp.transpose(vx[i]) for i in ids]
    bk = [jnp.concatenate([btx[i], ktx[i]], axis=0) for i in ids]

    s = {(d, p): s_ref[d, p] for d in range(N_DIR) for p in range(N_PAIR)}
    for t in range(n_sub):
        now = [i for i in ids if chains[i][2] == t]
        st = {i: jnp.transpose(s[chains[i][:2]]) for i in now}
        u = {i: _dot(wu[i][:, :PAIR_W], st[i]) + wu[i][:, PAIR_W:] for i in now}
        yx = {i: _dot(jnp.concatenate([rtx[i], a_rb[i]], axis=1), jnp.concatenate([st[i], u[i]], axis=0))
              + av[i][n2:] for i in now}
        ds = {i: _dot(jnp.concatenate([jnp.transpose(u[i]), vxt[i]], axis=1), bk[i]) for i in now}
        for i in now:
            d, p, _ = chains[i]
            s[d, p] = (s[d, p] + ds[i]) * q[d, t]['p_tot'][:, sl[i]]
            y_refs[d][rows_of(local(d, t)), sl[i]] = yx[i][:ck] + yx[i][ck:]
    for (d, p), val in s.items():
        s_ref[d, p] = val


def _rwkv_scan(lw, a, kd, kk, v, r, s0):
    _, b_, t_, w_ = lw.shape
    ck = _tile(t_, SCAN_CHUNK)
    rows = ck * math.gcd(t_ // ck, SCAN_CHUNKS_PER_STEP)
    nb = t_ // rows
    fwd = pl.BlockSpec((None, None, rows, w_), lambda b, c: (0, b, c, 0))
    bwd = pl.BlockSpec((None, None, rows, w_), lambda b, c: (1, b, nb - 1 - c, 0))
    fwd_shared = pl.BlockSpec((None, rows, w_), lambda b, c: (b, c, 0))
    bwd_shared = pl.BlockSpec((None, rows, w_), lambda b, c: (b, nb - 1 - c, 0))
    state_spec = pl.BlockSpec((N_DIR, None, N_PAIR, PAIR_W, PAIR_W), lambda b, c: (0, b, 0, 0, 0))
    y0, y1, s_fin = pl.pallas_call(
        functools.partial(_scan_kernel, chunk=ck),
        grid=(b_, nb),
        in_specs=[fwd, bwd, fwd, bwd, fwd, bwd, fwd_shared, bwd_shared, fwd_shared, bwd_shared,
                  fwd_shared, bwd_shared, state_spec],
        out_specs=[fwd_shared, bwd_shared, state_spec],
        out_shape=[jax.ShapeDtypeStruct((b_, t_, w_), jnp.float32), jax.ShapeDtypeStruct((b_, t_, w_), jnp.float32),
                   jax.ShapeDtypeStruct((N_DIR, b_, N_PAIR, PAIR_W, PAIR_W), jnp.float32)],
        compiler_params=_params("parallel", "arbitrary"),
        name="rwkv_scan",
    )(lw, lw, a, a, kd, kd, kk, kk, v, v, r, r, s0)
    return y0, y1, s_fin


def _rope_tables(n_tok):
    rows = n_tok // GRID_W
    row = jnp.broadcast_to(jnp.arange(rows)[:, None], (rows, GRID_W)).reshape(-1)
    col = jnp.broadcast_to(jnp.arange(GRID_W)[None, :], (rows, GRID_W)).reshape(-1)
    nf = B_HD // 4
    inv = ROPE_BASE ** (-jnp.arange(nf, dtype=jnp.float32) / nf)
    ang_r = row.astype(jnp.float32)[:, None] * inv[None, :]
    ang_c = col.astype(jnp.float32)[:, None] * inv[None, :]
    cos = jnp.concatenate([jnp.cos(ang_r)] * 2 + [jnp.cos(ang_c)] * 2, axis=1)
    sin = jnp.concatenate([-jnp.sin(ang_r), jnp.sin(ang_r), -jnp.sin(ang_c), jnp.sin(ang_c)], axis=1)
    return jnp.tile(cos, (1, 2)), jnp.tile(sin, (1, 2))


def _layer_tables(p, lam_init):
    w = _bf(p['w_in'])
    off, cols = 0, {}
    for name, size in zip(('dk', 'dv', 'kr', 'vr', 'wl', 'al', 'dq', 'r', 'u', 'va', 'za', 'zb', 'zc', 'gl'),
                          STATE_SIZES + OUT_SIZES):
        cols[name] = (off, off + size)
        off += size
    sl = lambda a, b=None: w[:, cols[a][0]:cols[b or a][1]]
    grp = jnp.arange(PAIR_W) // C_HD
    ones64 =_bf((grp[:, None] == grp[None, :]).astype(jnp.float32))
    lora_rows = jnp.arange(N_DIR * W_LORA) // W_LORA

    def padded(w2):
        stacked = jnp.concatenate([w2, w2], axis=1)
        return _bf(jnp.where((lora_rows[None, :, None] == jnp.arange(N_DIR)[:, None, None]), stacked, 0.0))

    row = lambda v: v.reshape(1, -1).astype(jnp.float32)
    return dict(
        wk=sl('dk'), wv=sl('dv'), wq=sl('dq'), ones64=ones64, mean64=_bf(ones64.astype(jnp.float32) / C_HD),
        gk=row(jnp.tile(p['d_knorm'], BR_W // B_HD)), gq=row(jnp.tile(p['d_qnorm'], BR_W // B_HD)),
        wkvr=jnp.concatenate([sl('kr'), sl('vr'), sl('r')], axis=1), wla=sl('wl', 'al'),
        taps=jnp.concatenate([p['r_conv'][1], p['r_conv'][2], p['r_conv'][0]], axis=1).reshape(3, 1, 3 * BR_W),
        r_kk=row(p['r_kk']), r_ka=row(p['r_ka']), r_rk=row(p['r_rk']),
        w2=padded(p['r_w2']), a2=padded(p['r_a2']),
        w0=p['r_w0'].reshape(N_DIR, 1, BR_W), a0=p['r_a0'].reshape(N_DIR, 1, BR_W),
        wgates=sl('u', 'zc'), a_ln_g=row(p['a_ln_g']), a_ln_b=row(p['a_ln_b']),
        wgl=sl('gl'), wbr=_bf(p['w_br']), wout=_bf(p['w_out']),
        r_ln_g=row(p['r_ln_g']), r_ln_b=row(p['r_ln_b']),
        subln_g=row(p['d_subln_g'] * (1.0 - lam_init)),
        a_ws=_bf(p['a_ws']),
        a_bias=jnp.repeat(jnp.swapaxes(p['a_bs'], 0, 1), A_GW, axis=1).astype(jnp.float32),
    )


def _stream(x, gs, sh, gate, lw, p, rope, lam, ctx_kv, s0, need_out):
    b_, t_, d = x.shape
    n = b_ * t_
    x2 = x.reshape(n, d)
    k_att, v_att, q = _qkv_proj(x2, t_, gs, sh, lw, rope)
    kk, v, r, kd, lwd, a, bonus = _rwkv_in_proj(x2, t_, gs, sh, lw)
    as3 = lambda z: z.reshape(b_, t_, BR_W)
    as4 = lambda z: z.reshape(N_DIR, b_, t_, BR_W)
    k_att, v_att = as3(k_att), as3(v_att)
    y0, y1, s_fin = _rwkv_scan(as4(lwd), as4(a), as4(kd), as3(kk), as3(v), as3(r), s0)
    if not need_out:
        return None, (k_att, v_att), s_fin
    uz, vn, szb, szc = _gates_proj(x2, t_, gs, sh, lw)
    ya = _gmlp_branch(vn, uz, lw['a_ws'], lw['a_bias'])
    ks = [k_att] + ([ctx_kv[0]] if ctx_kv is not None else [])
    vs = [v_att] + ([ctx_kv[1]] if ctx_kv is not None else [])
    yb = _diff_attention(as3(q), as3(szb), lw['subln_g'], ks, vs, lam).reshape(n, BR_W)
    out = _merge(x2, t_, gs, sh, gate, ya, yb, y0.reshape(n, BR_W), y1.reshape(n, BR_W), bonus, szc, lw)
    return out.reshape(b_, t_, d), (k_att, v_att), s_fin


def _layer(x, xc, c_act, cc_act, rope, lam_init, p, update_ctx):
    d = D_MODEL
    b_ = x.shape[0]
    lw = _layer_tables(p, lam_init)
    mod_all = _mod_rows(jnp.concatenate([c_act, cc_act[None]], axis=0), p['w_mod'], p['b_mod'])
    mod, mod_c = mod_all[:b_], mod_all[b_:]
    rows3 = lambda z: z[:, None, :]
    g = p['norm_g']
    lp = p['d_lam']
    lam = jnp.exp(jnp.sum(lp[0] * lp[1])) - jnp.exp(jnp.sum(lp[2] * lp[3])) + lam_init
    s_zero = jnp.zeros((N_DIR, b_, N_PAIR, PAIR_W, PAIR_W), jnp.float32)

    xc_next, ctx_kv, s_ctx = _stream(xc, rows3(g * (1.0 + mod_c[:, d:2 * d])), rows3(mod_c[:, :d]),
                                     rows3(mod_c[:, 2 * d:]), lw, p, None, lam, None, s_zero, update_ctx)
    x_next, _, _ = _stream(x, rows3(g * (1.0 + mod[:, d:2 * d])), rows3(mod[:, :d]), rows3(mod[:, 2 * d:]),
                           lw, p, rope, lam, ctx_kv, s_ctx, True)
    return x_next, xc_next


def kernel(x, c, ctx, c_ctx, w_mod, b_mod, norm_g, w_in, a_ln_g, a_ln_b, a_ws, a_bs, d_qnorm, d_knorm, d_lam,
           d_subln_g, r_conv, r_w0, r_w2, r_a0, r_a2, r_kk, r_ka, r_rk, r_ln_g, r_ln_b, w_br, w_out):
    rope = _rope_tables(x.shape[1])
    c_act = jax.nn.silu(c)
    cc_act = jax.nn.silu(c_ctx)
    xc = ctx
    depth = w_in.shape[0]
    for l in range(depth):
        p = dict(w_mod=w_mod[l], b_mod=b_mod[l], norm_g=norm_g[l], w_in=w_in[l],
                 a_ln_g=a_ln_g[l], a_ln_b=a_ln_b[l], a_ws=a_ws[l], a_bs=a_bs[l],
                 d_qnorm=d_qnorm[l], d_knorm=d_knorm[l], d_lam=d_lam[l], d_subln_g=d_subln_g[l],
                 r_conv=r_conv[l], r_w0=r_w0[l], r_w2=r_w2[l], r_a0=r_a0[l], r_a2=r_a2[l],
                 r_kk=r_kk[l], r_ka=r_ka[l], r_rk=r_rk[l], r_ln_g=r_ln_g[l], r_ln_b=r_ln_b[l],
                 w_br=w_br[l], w_out=w_out[l])
        lam_init = 0.8 - 0.6 * math.exp(-0.3 * l)
        x, xc = _layer(x, xc, c_act, cc_act, rope, lam_init, p, l < depth - 1)
    return x
```

```python
import functools
import math

import jax
import jax.numpy as jnp
from jax import lax
from jax.experimental import pallas as pl
from jax.experimental.pallas import tpu as pltpu

D_MODEL = 1024
GRID_W = 64
N_BRANCH = 3
BR_W = D_MODEL // 2
A_GROUPS = 4
A_GW = BR_W // A_GROUPS
A_CHUNK = 128
B_HD = 64
B_VD = 2 * B_HD
B_HEADS = BR_W // B_VD
ROPE_BASE = 10000.0
C_HD = 64
C_HEADS = BR_W // C_HD
N_DIR = 2
W_LORA = 64
A_LORA = 64
NORM_EPS = 1e-6
LN_EPS = 1e-5
GN_EPS = 64e-5
STATE_SIZES = (BR_W, BR_W, BR_W, BR_W, N_DIR * W_LORA, N_DIR * A_LORA)
OUT_SIZES = (BR_W,) * 7 + (N_BRANCH * D_MODEL,)
STATE_COLS = sum(STATE_SIZES)
OUT_COLS = sum(OUT_SIZES)

ROW_TILE = 512
HALO_ROWS = 16
SCAN_CHUNK = 64
SCAN_CHUNKS_PER_STEP = 2
ATTN_TQ = 512
ATTN_ROWS = 256
ATTN_UNROLL = 8
ATTN_SAFE_BOUND = 60.0
VMEM_LIMIT_BYTES = 48 * 1024 * 1024
LOG2E = 1.4426950408889634
DECAY_SCALE = math.exp(-0.5)


def _tile(n, pref):
    if n <= pref:
        return n
    t = pref
    while n % t:
        t //= 2
    return t


def _bf(x):
    return x.astype(jnp.bfloat16)


def _dot(a, b):
    return jnp.dot(_bf(a), _bf(b), preferred_element_type=jnp.float32)


def _dot_nt(a, b):
    return lax.dot_general(_bf(a), _bf(b), (((1,), (1,)), ((), ())), preferred_element_type=jnp.float32)


def _split(x):
    hi = _bf(x)
    return hi, _bf(x - hi.astype(jnp.float32))


def _dot3_wide(a, b):
    ah, al = _split(a)
    bh, bl = _split(b)
    n = b.shape[1]
    f = lambda x, y: jnp.dot(x, y, preferred_element_type=jnp.float32)
    wide = f(ah, jnp.concatenate([bh, bl], axis=1))
    return wide[:, :n] + (wide[:, n:] + f(al, bh))


def _group_sum(x, ones_ref, exact=False):
    g = ones_ref[...]
    f = lambda a: jnp.dot(a, g, preferred_element_type=jnp.float32)
    blocks = []
    for c0 in range(0, x.shape[1], PAIR_W):
        xb = x[:, c0:c0 + PAIR_W]
        if exact:
            hi, lo = _split(xb)
            blocks.append(f(hi) + f(lo))
        else:
            blocks.append(f(_bf(xb)))
    return jnp.concatenate(blocks, axis=1)


def _params(*semantics):
    return pltpu.CompilerParams(dimension_semantics=semantics, vmem_limit_bytes=VMEM_LIMIT_BYTES)


def _mod_kernel(a_ref, w_ref, b_ref, o_ref):
    o_ref[...] = jnp.dot(a_ref[...], w_ref[...], preferred_element_type=jnp.float32) + b_ref[...]


def _mod_rows(act, w_mod, b_mod):
    r = act.shape[0]
    a = _bf(jnp.pad(act, ((0, -r % 16), (0, 0))))
    w = _bf(w_mod)
    b = b_mod.reshape(1, -1)
    out = pl.pallas_call(
        _mod_kernel,
        grid=(1,),
        in_specs=[_const_spec(a), _const_spec(w), _const_spec(b)],
        out_specs=pl.BlockSpec((a.shape[0], w.shape[1]), lambda i: (0, 0)),
        out_shape=jax.ShapeDtypeStruct((a.shape[0], w.shape[1]), jnp.float32),
        compiler_params=_params("arbitrary"), name="mod_rows",
    )(a, w, b)
    return out[:r]


def _modulated(x, gs, sh):
    rinv = lax.rsqrt(jnp.mean(x * x, axis=-1, keepdims=True) + NORM_EPS)
    return _bf(x * rinv * gs + sh)


def _qkv_kernel(x_ref, gs_ref, sh_ref, wk_ref, wv_ref, wq_ref, ones_ref, gk_ref, gq_ref, *rest, rope):
    if rope:
        cos_ref, sin_ref, k_out, v_out, q_out = rest
    else:
        k_out, v_out, q_out = rest
    h = _modulated(x_ref[...], gs_ref[...], sh_ref[...])

    if rope:
        cos = jnp.concatenate([cos_ref[...]] * B_HEADS, axis=1)
        sin = jnp.concatenate([sin_ref[...]] * B_HEADS, axis=1)
        lane = lax.broadcasted_iota(jnp.int32, (1, BR_W), 1)
        first = (lane % (B_HD // 2)) < (B_HD // 4)

    def normed(z, g_ref):
        ss = _group_sum(z * z, ones_ref)
        y = z * lax.rsqrt(ss * (1.0 / B_HD) + NORM_EPS) * g_ref[...]
        if rope:
            partner = jnp.where(first, pltpu.roll(y, BR_W - B_HD // 4, 1), pltpu.roll(y, B_HD // 4, 1))
            y = y * cos + partner * sin
        return y

    zk = jnp.dot(h, wk_ref[...], preferred_element_type=jnp.float32)
    zq = jnp.dot(h, wq_ref[...], preferred_element_type=jnp.float32)
    v_out[...] = _bf(jnp.dot(h, wv_ref[...], preferred_element_type=jnp.float32))
    k_out[...] = _bf(normed(zk, gk_ref))
    q_out[...] = _bf(normed(zq, gq_ref) * (B_HD ** -0.5 * LOG2E))


def _rwkv_in_kernel(x_ref, xp_ref, xn_ref, gs_ref, sh_ref, w_ref, wla_ref, w2_ref, a2_ref, w0_ref, a0_ref,
                    taps_ref, rkk_ref, rka_ref, rrk_ref, ones_ref,
                    kk_out, v_out, r_out, kd_out, lw_out, a_out, bonus_out, *, per_batch):
    i = pl.program_id(0)
    gs, sh = gs_ref[...], sh_ref[...]
    h = _modulated(x_ref[...], gs, sh)
    w = w_ref[...]
    zp = jnp.dot(_modulated(xp_ref[...], gs, sh), w, preferred_element_type=jnp.float32)[HALO_ROWS - 1:HALO_ROWS]
    zn = jnp.dot(_modulated(xn_ref[...], gs, sh), w, preferred_element_type=jnp.float32)[0:1]
    zp = jnp.where(i % per_batch == 0, 0.0, zp)
    zn = jnp.where(i % per_batch == per_batch - 1, 0.0, zn)
    la = jnp.dot(h, wla_ref[...], preferred_element_type=jnp.float32)
    n_l = N_DIR * W_LORA
    wl = _bf(jnp.tanh(la[:, :n_l]))
    al = _bf(la[:, n_l:])
    w_log = [w0_ref[d] + jnp.dot(wl, w2_ref[d], preferred_element_type=jnp.float32) for d in range(N_DIR)]
    a_log = [a0_ref[d] + jnp.dot(al, a2_ref[d], preferred_element_type=jnp.float32) for d in range(N_DIR)]
    z = jnp.dot(h, w, preferred_element_type=jnp.float32)
    tm = z.shape[0]
    rowi = lax.broadcasted_iota(jnp.int32, (tm, 1), 0)
    before = jnp.where(rowi == 0, zp, pltpu.roll(z, 1, 0))
    after = jnp.where(rowi == tm - 1, zn, pltpu.roll(z, tm - 1, 0))
    z = before * taps_ref[0] + z * taps_ref[1] + after * taps_ref[2]
    k, v, r = z[:, :BR_W], z[:, BR_W:2 * BR_W], z[:, 2 * BR_W:]
    v_out[...] = v
    r_out[...] = r
    kk = k * rkk_ref[...]
    kk_out[...] = kk / jnp.maximum(jnp.sqrt(_group_sum(kk * kk, ones_ref)), 1e-12)
    bonus_out[...] = _group_sum(r * k * rrk_ref[...], ones_ref) * v

    for d in range(N_DIR):
        lw_out[d] = -DECAY_SCALE * jax.nn.sigmoid(w_log[d])
        a = jax.nn.sigmoid(a_log[d])
        a_out[d] = a
        kd_out[d] = k * (1.0 + (a - 1.0) * rka_ref[...])


def _gates_kernel(x_ref, gs_ref, sh_ref, w_ref, lng_ref, lnb_ref, ws_ref, bias_ref, ya_out, szb_out, szc_out):
    h = _modulated(x_ref[...], gs_ref[...], sh_ref[...])
    z = jnp.dot(h, w_ref[...], preferred_element_type=jnp.float32)
    u, va, za, zb, zc = (z[:, i * BR_W:(i + 1) * BR_W] for i in range(5))
    szb_out[...] = _bf(jax.nn.silu(zb))
    szc_out[...] = _bf(jax.nn.silu(zc))
    uz = jax.nn.gelu(u) * jax.nn.silu(za)
    va = jax.nn.gelu(va)
    mu = jnp.mean(va, axis=-1, keepdims=True)
    var = jnp.mean(jnp.square(va - mu), axis=-1, keepdims=True)
    vn = _bf((va - mu) * lax.rsqrt(var + LN_EPS) * lng_ref[...] + lnb_ref[...])
    for r0 in range(0, vn.shape[0], A_CHUNK):
        for g in range(A_GROUPS):
            c0 = g * A_GW
            s = jnp.dot(ws_ref[g], vn[r0:r0 + A_CHUNK, c0:c0 + A_GW], preferred_element_type=jnp.float32)
            ya_out[r0:r0 + A_CHUNK, c0:c0 + A_GW] = _bf(uz[r0:r0 + A_CHUNK, c0:c0 + A_GW]
                                                        * (s + bias_ref[:, c0:c0 + A_GW]))


def _row_specs(n, t_, n_mod):
    tm = _tile(t_, ROW_TILE)
    per_batch = t_ // tm
    x_spec = pl.BlockSpec((tm, D_MODEL), lambda i: (i, 0))
    mod_spec = pl.BlockSpec((None, 1, D_MODEL), (lambda i: (i // per_batch, 0, 0)) if n_mod > 1 else (lambda i: (0, 0, 0)))
    return tm, per_batch, x_spec, mod_spec


def _const_spec(a):
    nd = a.ndim
    return pl.BlockSpec(a.shape, lambda i: (0,) * nd)


def _out_spec(tm, w):
    return pl.BlockSpec((tm, w), lambda i: (i, 0))


def _qkv_proj(x2, t_, gs, sh, lw, rope):
    n = x2.shape[0]
    tm, per_batch, x_spec, mod_spec = _row_specs(n, t_, gs.shape[0])
    consts = [lw['wk'], lw['wv'], lw['wq'], lw['ones64'], lw['gk'], lw['gq']]
    in_specs = [x_spec, mod_spec, mod_spec] + [_const_spec(a) for a in consts]
    args = [x2, gs, sh] + consts
    if rope is not None:
        rope_spec = pl.BlockSpec((tm, B_VD), lambda i: (i % per_batch, 0))
        in_specs += [rope_spec, rope_spec]
        args += list(rope)
    out = jax.ShapeDtypeStruct((n, BR_W), jnp.bfloat16)
    return pl.pallas_call(
        functools.partial(_qkv_kernel, rope=rope is not None),
        grid=(n // tm,), in_specs=in_specs, out_specs=[_out_spec(tm, BR_W)] * 3, out_shape=[out] * 3,
        compiler_params=_params("parallel"), name="qkv_proj",
    )(*args)


def _rwkv_in_proj(x2, t_, gs, sh, lw):
    n = x2.shape[0]
    tm, per_batch, x_spec, mod_spec = _row_specs(n, t_, gs.shape[0])
    halo_per_tile = tm // HALO_ROWS
    prev_spec = pl.BlockSpec((HALO_ROWS, D_MODEL), lambda i: (jnp.maximum(i * halo_per_tile - 1, 0), 0))
    next_spec = pl.BlockSpec((HALO_ROWS, D_MODEL),
                             lambda i: (jnp.minimum((i + 1) * halo_per_tile, n // HALO_ROWS - 1), 0))
    consts = [lw['wkvr'], lw['wla'], lw['w2'], lw['a2'], lw['w0'], lw['a0'], lw['taps'], lw['r_kk'], lw['r_ka'],
              lw['r_rk'], lw['ones64']]
    f32 = jax.ShapeDtypeStruct((n, BR_W), jnp.float32)
    f32d = jax.ShapeDtypeStruct((N_DIR, n, BR_W), jnp.float32)
    row_spec = _out_spec(tm, BR_W)
    dir_spec = pl.BlockSpec((N_DIR, tm, BR_W), lambda i: (0, i, 0))
    return pl.pallas_call(
        functools.partial(_rwkv_in_kernel, per_batch=per_batch),
        grid=(n // tm,),
        in_specs=[x_spec, prev_spec, next_spec, mod_spec, mod_spec] + [_const_spec(a) for a in consts],
        out_specs=[row_spec] * 3 + [dir_spec] * 3 + [row_spec],
        out_shape=[f32] * 3 + [f32d] * 3 + [f32],
        compiler_params=_params("parallel"), name="rwkv_in_proj",
    )(x2, x2, x2, gs, sh, *consts)


def _gates_proj(x2, t_, gs, sh, lw):
    n = x2.shape[0]
    tm, _, x_spec, mod_spec = _row_specs(n, t_, gs.shape[0])
    consts = [lw['wgates'], lw['a_ln_g'], lw['a_ln_b'], lw['a_ws'], lw['a_bias']]
    out = jax.ShapeDtypeStruct((n, BR_W), jnp.bfloat16)
    return pl.pallas_call(
        _gates_kernel,
        grid=(n // tm,), in_specs=[x_spec, mod_spec, mod_spec] + [_const_spec(a) for a in consts],
        out_specs=[_out_spec(tm, BR_W)] * 3, out_shape=[out] * 3,
        compiler_params=_params("parallel"), name="gates_proj",
    )(x2, gs, sh, *consts)


def _merge_kernel(x_ref, gs_ref, sh_ref, gate_ref, ya_ref, yb_ref, y0_ref, y1_ref, bonus_ref, szc_ref,
                  mean_ref, lng_ref, lnb_ref, wgl_ref, wbr_ref, wout_ref, o_ref):
    x = x_ref[...]
    y = y0_ref[...] + y1_ref[...]
    dev = y - _group_sum(y, mean_ref, exact=True)
    var = _group_sum(dev * dev, mean_ref)
    yc = (dev * lax.rsqrt(var + GN_EPS) * lng_ref[...] + lnb_ref[...] + bonus_ref[...]) * szc_ref[...].astype(jnp.float32)
    h = _modulated(x, gs_ref[...], sh_ref[...])
    g = jax.nn.sigmoid(jnp.dot(h, wgl_ref[...], preferred_element_type=jnp.float32))
    ys = (ya_ref[...], yb_ref[...], _bf(yc))
    mix = None
    for i in range(N_BRANCH):
        up = jnp.dot(ys[i], wbr_ref[i], preferred_element_type=jnp.float32)
        term = g[:, i * D_MODEL:(i + 1) * D_MODEL] * up
        mix = term if mix is None else mix + term
    o_ref[...] = x + gate_ref[...] * jnp.dot(_bf(mix), wout_ref[...], preferred_element_type=jnp.float32)


def _merge(x2, t_, gs, sh, gate, ya, yb, y0, y1, bonus, szc, lw):
    n = x2.shape[0]
    tm, _, x_spec, mod_spec = _row_specs(n, t_, gs.shape[0])
    consts = [lw['mean64'], lw['r_ln_g'], lw['r_ln_b'], lw['wgl'], lw['wbr'], lw['wout']]
    br = _out_spec(tm, BR_W)
    return pl.pallas_call(
        _merge_kernel,
        grid=(n // tm,),
        in_specs=[x_spec, mod_spec, mod_spec, mod_spec] + [br] * 6 + [_const_spec(a) for a in consts],
        out_specs=_out_spec(tm, D_MODEL), out_shape=jax.ShapeDtypeStruct((n, D_MODEL), jnp.float32),
        compiler_params=_params("parallel"), name="merge_out",
    )(x2, gs, sh, gate, ya, yb, y0, y1, bonus, szc, *consts)


def _attn_kernel(lam_ref, q_ref, szb_ref, g_ref, *refs, chunks):
    n_src = len(chunks)
    k_refs = refs[:n_src]
    v_refs = refs[n_src:2 * n_src]
    o_ref = refs[2 * n_src]
    kmax_scr = refs[2 * n_src + 1]
    q = q_ref[0]
    tq = q.shape[0]
    rows = min(tq, ATTN_ROWS)
    lane = lax.broadcasted_iota(jnp.int32, (1, B_VD), 1)
    sub = (lane < B_HD, lane >= B_HD)
    zero = jnp.zeros_like(q)
    q_sub = [jnp.where(mask, q, zero) for mask in sub]
    qs = [qj[r:r + rows] for r in range(0, tq, rows) for qj in q_sub]

    @pl.when(pl.program_id(2) == 0)
    def _():
        for j, mask in enumerate(sub):
            best = jnp.zeros((1, 1), jnp.float32)
            for k_ref in k_refs:
                kf = k_ref[0].astype(jnp.float32)
                norm2 = jnp.sum(jnp.where(mask, kf * kf, 0.0), axis=-1, keepdims=True)
                best = jnp.maximum(best, jnp.max(norm2, axis=0, keepdims=True))
            kmax_scr[j] = jnp.broadcast_to(best, kmax_scr.shape[1:])

    qf = q.astype(jnp.float32)
    bound_sub = [jnp.sqrt(jnp.sum(jnp.where(mask, qf * qf, 0.0), axis=-1, keepdims=True) * kmax_scr[j][0:1, 0:1])
                 for j, mask in enumerate(sub)]
    bounds = [bj[r:r + rows] for r in range(0, tq, rows) for bj in bound_sub]
    worst = jnp.max(jnp.maximum(bound_sub[0], bound_sub[1]))

    def scores(k):
        return [lax.dot_general(qj, k, (((1,), (1,)), ((), ())), preferred_element_type=jnp.float32) for qj in qs]

    def fixed_update(carry, k, v):
        ps = [jnp.exp2(s - b) for s, b in zip(scores(k), bounds)]
        pv = [jnp.dot(p.astype(v.dtype), v, preferred_element_type=jnp.float32) for p in ps]
        return tuple((l + jnp.sum(p, axis=-1, keepdims=True), acc + o) for (l, acc), p, o in zip(carry, ps, pv))

    def online_update(carry, k, v):
        ss = scores(k)
        m_new = [jnp.maximum(m, jnp.max(s, axis=-1, keepdims=True)) for s, (m, _, _) in zip(ss, carry)]
        ps = [jnp.exp2(s - mn) for s, mn in zip(ss, m_new)]
        pv = [jnp.dot(p.astype(v.dtype), v, preferred_element_type=jnp.float32) for p in ps]
        new = []
        for j, (m, l, acc) in enumerate(carry):
            alpha = jnp.exp2(m - m_new[j])
            new.append((m_new[j], alpha * l + jnp.sum(ps[j], axis=-1, keepdims=True), alpha * acc + pv[j]))
        return tuple(new)

    def over_keys(update, carry):
        for k_ref, v_ref, (n_chunk, tk) in zip(k_refs, v_refs, chunks):
            def body(i, c, k_ref=k_ref, v_ref=v_ref, tk=tk):
                off = pl.multiple_of(i * tk, tk)
                return update(c, k_ref[0, pl.ds(off, tk), :], v_ref[0, pl.ds(off, tk), :])
            carry = lax.fori_loop(0, n_chunk, body, carry, unroll=math.gcd(n_chunk, ATTN_UNROLL))
        return carry

    def finish(sums):
        lam = lam_ref[0]
        for b in range(tq // rows):
            (l1, acc1), (l2, acc2) = sums[2 * b], sums[2 * b + 1]
            att = acc1 / l1 - lam * (acc2 / l2)
            att = att * lax.rsqrt(jnp.mean(att * att, axis=-1, keepdims=True) + NORM_EPS) * g_ref[...]
            o_ref[0, b * rows:(b + 1) * rows, :] = _bf(att * szb_ref[0, b * rows:(b + 1) * rows, :].astype(jnp.float32))

    zeros = lambda w: jnp.zeros((rows, w), jnp.float32)

    @pl.when(worst <= ATTN_SAFE_BOUND)
    def _():
        finish(over_keys(fixed_update, tuple((zeros(1), zeros(B_VD)) for _ in qs)))

    @pl.when(worst > ATTN_SAFE_BOUND)
    def _():
        carry = over_keys(online_update, tuple((jnp.full((rows, 1), -jnp.inf, jnp.float32), zeros(1), zeros(B_VD))
                                               for _ in qs))
        finish([(l, acc) for _, l, acc in carry])


def _diff_attention(q, szb, subln_g, ks, vs, lam):
    b_, t_, _ = q.shape
    tq = _tile(t_, ATTN_TQ)
    chunks = tuple((k.shape[1] // _tile(k.shape[1], 512), _tile(k.shape[1], 512)) for k in ks)
    kv_specs = [pl.BlockSpec((1, k.shape[1], B_VD), lambda b, h, i: (b, 0, h)) for k in ks]
    q_spec = pl.BlockSpec((1, tq, B_VD), lambda b, h, i: (b, i, h))
    return pl.pallas_call(
        functools.partial(_attn_kernel, chunks=chunks),
        grid=(b_, B_HEADS, t_ // tq),
        in_specs=[pl.BlockSpec(memory_space=pltpu.SMEM), q_spec, q_spec,
                  pl.BlockSpec((1, B_VD), lambda b, h, i: (0, 0))] + kv_specs + kv_specs,
        out_specs=q_spec,
        out_shape=jax.ShapeDtypeStruct((b_, t_, BR_W), jnp.bfloat16),
        scratch_shapes=[pltpu.VMEM((2, 8, B_VD), jnp.float32)],
        compiler_params=_params("parallel", "parallel", "arbitrary"),
        name="diff_attention",
    )(lam.reshape(1).astype(jnp.float32), q, szb, subln_g, *ks, *vs)


N_PAIR = C_HEADS // 2
PAIR_W = 2 * C_HD


def _scan_kernel(lw0_ref, lw1_ref, a0_ref, a1_ref, kd0_ref, kd1_ref, kk0_ref, kk1_ref, v0_ref, v1_ref,
                 r0_ref, r1_ref, s0_ref, y0_ref, y1_ref, s_ref, *, chunk):
    @pl.when(pl.program_id(1) == 0)
    def _():
        s_ref[...] = s0_ref[...]

    ck = chunk
    n_sub = lw0_ref.shape[0] // ck
    n2 = 2 * ck
    row = lax.broadcasted_iota(jnp.int32, (n2, n2), 0)
    col = lax.broadcasted_iota(jnp.int32, (n2, n2), 1)
    gap = row % ck - col % ck
    strict = (gap > 0, gap < 0)
    incl = (gap >= 0, gap <= 0)
    eye = (row == col).astype(jnp.float32)
    gap1 = lax.broadcasted_iota(jnp.int32, (ck, ck), 0) - lax.broadcasted_iota(jnp.int32, (ck, ck), 1)
    incl1 = (_bf((gap1 >= 0).astype(jnp.float32)), _bf((gap1 <= 0).astype(jnp.float32)))
    head0 = lax.broadcasted_iota(jnp.int32, (1, PAIR_W), 1) < C_HD

    def stack(x):
        return jnp.concatenate([jnp.where(head0, x, 0.0), jnp.where(head0, 0.0, x)], axis=0)

    def scaled(d, rows, lw_ref, a_ref, kd_ref, kk_ref, v_ref, r_ref):
        lw = lw_ref[rows, :]
        lw_hi, lw_lo = _split(lw)
        cum = (jnp.dot(incl1[d], lw_hi, preferred_element_type=jnp.float32)
               + jnp.dot(incl1[d], lw_lo, preferred_element_type=jnp.float32))
        p_in = jnp.exp(cum)
        p_inv = jnp.exp(-cum)
        kk = kk_ref[rows, :]
        return dict(at=-kk * jnp.exp(cum - lw), bt=kk * a_ref[rows, :] * p_inv, kt=kd_ref[rows, :] * p_inv,
                    rt=r_ref[rows, :] * p_in, v=v_ref[rows, :], p_tot=jnp.exp(jnp.sum(lw, axis=0, keepdims=True)))

    in_refs = ((lw0_ref, a0_ref, kd0_ref, kk0_ref, v0_ref, r0_ref), (lw1_ref, a1_ref, kd1_ref, kk1_ref, v1_ref, r1_ref))
    y_refs = (y0_ref, y1_ref)
    local = lambda d, t: t if d == 0 else n_sub - 1 - t
    rows_of = lambda c: slice(c * ck, (c + 1) * ck)
    q = {(d, t): scaled(d, rows_of(local(d, t)), *in_refs[d]) for t in range(n_sub) for d in range(N_DIR)}

    chains = [(d, p, t) for t in range(n_sub) for p in range(N_PAIR) for d in range(N_DIR)]
    ids = range(len(chains))
    sl = [slice(p * PAIR_W, (p + 1) * PAIR_W) for _, p, _ in chains]
    atx, btx, ktx, rtx, vx = ([stack(q[d, t][name][:, sl[i]]) for i, (d, _, t) in enumerate(chains)]
                              for name in ('at', 'bt', 'kt', 'rt', 'v'))
    big = [_dot_nt(jnp.concatenate([atx[i], rtx[i]], axis=0), jnp.concatenate([btx[i], ktx[i]], axis=0))
           for i in ids]
    a_ab = [jnp.where(strict[chains[i][0]], big[i][:n2, :n2], 0.0) for i in ids]
    a_ak = [jnp.where(strict[chains[i][0]], big[i][:n2, n2:], 0.0) for i in ids]
    a_rb = [jnp.where(incl[chains[i][0]], big[i][n2:, :n2], 0.0) for i in ids]
    a_rk = [jnp.where(incl[chains[i][0]], big[i][n2:, n2:], 0.0) for i in ids]

    tinv = [eye + a_ab[i] for i in ids]
    apow = [_dot(a_ab[i], a_ab[i]) for i in ids]
    span = 2
    while span < ck:
        if 2 * span < ck:
            both = [_dot(apow[i], jnp.concatenate([apow[i], tinv[i]], axis=1)) for i in ids]
            apow = [both[i][:, :n2] for i in ids]
            tinv = [tinv[i] + both[i][:, n2:] for i in ids]
        else:
            tinv = [tinv[i] + _dot(apow[i], tinv[i]) for i in ids]
        span *= 2
    resid = [(eye - tinv[i]) + _dot3_wide(a_ab[i], tinv[i]) for i in ids]
    tinv = [tinv[i] + _dot(tinv[i], resid[i]) for i in ids]

    av = [_dot(jnp.concatenate([a_ak[i], a_rk[i]], axis=0), vx[i]) for i in ids]
    wu = [_dot(tinv[i], jnp.concatenate([atx[i], av[i][:n2]], axis=1)) for i in ids]
    vxt = [jnp.transpose(vx[i]) for i in ids]
    bk = [jnp.concatenate([btx[i], ktx[i]], axis=0) for i in ids]

    s = {(d, p): s_ref[d, p] for d in range(N_DIR) for p in range(N_PAIR)}
    for t in range(n_sub):
        now = [i for i in ids if chains[i][2] == t]
        st = {i: jnp.transpose(s[chains[i][:2]]) for i in now}
        u = {i: _dot(wu[i][:, :PAIR_W], st[i]) + wu[i][:, PAIR_W:] for i in now}
        yx = {i: _dot(jnp.concatenate([rtx[i], a_rb[i]], axis=1), jnp.concatenate([st[i], u[i]], axis=0))
              + av[i][n2:] for i in now}
        ds = {i: _dot(jnp.concatenate([jnp.transpose(u[i]), vxt[i]], axis=1), bk[i]) for i in now}
        for i in now:
            d, p, _ = chains[i]
            s[d, p] = (s[d, p] + ds[i]) * q[d, t]['p_tot'][:, sl[i]]
            y_refs[d][rows_of(local(d, t)), sl[i]] = yx[i][:ck] + yx[i][ck:]
    for (d, p), val in s.items():
        s_ref[d, p] = val


def _rwkv_scan(lw, a, kd, kk, v, r, s0):
    _, b_, t_, w_ = lw.shape
    ck = _tile(t_, SCAN_CHUNK)
    rows = ck * math.gcd(t_ // ck, SCAN_CHUNKS_PER_STEP)
    nb = t_ // rows
    fwd = pl.BlockSpec((None, None, rows, w_), lambda b, c: (0, b, c, 0))
    bwd = pl.BlockSpec((None, None, rows, w_), lambda b, c: (1, b, nb - 1 - c, 0))
    fwd_shared = pl.BlockSpec((None, rows, w_), lambda b, c: (b, c, 0))
    bwd_shared = pl.BlockSpec((None, rows, w_), lambda b, c: (b, nb - 1 - c, 0))
    state_spec = pl.BlockSpec((N_DIR, None, N_PAIR, PAIR_W, PAIR_W), lambda b, c: (0, b, 0, 0, 0))
    y0, y1, s_fin = pl.pallas_call(
        functools.partial(_scan_kernel, chunk=ck),
        grid=(b_, nb),
        in_specs=[fwd, bwd, fwd, bwd, fwd, bwd, fwd_shared, bwd_shared, fwd_shared, bwd_shared,
                  fwd_shared, bwd_shared, state_spec],
        out_specs=[fwd_shared, bwd_shared, state_spec],
        out_shape=[jax.ShapeDtypeStruct((b_, t_, w_), jnp.float32), jax.ShapeDtypeStruct((b_, t_, w_), jnp.float32),
                   jax.ShapeDtypeStruct((N_DIR, b_, N_PAIR, PAIR_W, PAIR_W), jnp.float32)],
        compiler_params=_params("parallel", "arbitrary"),
        name="rwkv_scan",
    )(lw, lw, a, a, kd, kd, kk, kk, v, v, r, r, s0)
    return y0, y1, s_fin


def _rope_tables(n_tok):
    rows = n_tok // GRID_W
    row = jnp.broadcast_to(jnp.arange(rows)[:, None], (rows, GRID_W)).reshape(-1)
    col = jnp.broadcast_to(jnp.arange(GRID_W)[None, :], (rows, GRID_W)).reshape(-1)
    nf = B_HD // 4
    inv = ROPE_BASE ** (-jnp.arange(nf, dtype=jnp.float32) / nf)
    ang_r = row.astype(jnp.float32)[:, None] * inv[None, :]
    ang_c = col.astype(jnp.float32)[:, None] * inv[None, :]
    cos = jnp.concatenate([jnp.cos(ang_r)] * 2 + [jnp.cos(ang_c)] * 2, axis=1)
    sin = jnp.concatenate([-jnp.sin(ang_r), jnp.sin(ang_r), -jnp.sin(ang_c), jnp.sin(ang_c)], axis=1)
    return jnp.tile(cos, (1, 2)), jnp.tile(sin, (1, 2))


def _layer_tables(p, lam_init):
    w = _bf(p['w_in'])
    off, cols = 0, {}
    for name, size in zip(('dk', 'dv', 'kr', 'vr', 'wl', 'al', 'dq', 'r', 'u', 'va', 'za', 'zb', 'zc', 'gl'),
                          STATE_SIZES + OUT_SIZES):
        cols[name] = (off, off + size)
        off += size
    sl = lambda a, b=None: w[:, cols[a][0]:cols[b or a][1]]
    grp = jnp.arange(PAIR_W) // C_HD
    ones64 =_bf((grp[:, None] == grp[None, :]).astype(jnp.float32))
    lora_rows = jnp.arange(N_DIR * W_LORA) // W_LORA

    def padded(w2):
        stacked = jnp.concatenate([w2, w2], axis=1)
        return _bf(jnp.where((lora_rows[None, :, None] == jnp.arange(N_DIR)[:, None, None]), stacked, 0.0))

    row = lambda v: v.reshape(1, -1).astype(jnp.float32)
    return dict(
        wk=sl('dk'), wv=sl('dv'), wq=sl('dq'), ones64=ones64, mean64=_bf(ones64.astype(jnp.float32) / C_HD),
        gk=row(jnp.tile(p['d_knorm'], BR_W // B_HD)), gq=row(jnp.tile(p['d_qnorm'], BR_W // B_HD)),
        wkvr=jnp.concatenate([sl('kr'), sl('vr'), sl('r')], axis=1), wla=sl('wl', 'al'),
        taps=jnp.concatenate([p['r_conv'][1], p['r_conv'][2], p['r_conv'][0]], axis=1).reshape(3, 1, 3 * BR_W),
        r_kk=row(p['r_kk']), r_ka=row(p['r_ka']), r_rk=row(p['r_rk']),
        w2=padded(p['r_w2']), a2=padded(p['r_a2']),
        w0=p['r_w0'].reshape(N_DIR, 1, BR_W), a0=p['r_a0'].reshape(N_DIR, 1, BR_W),
        wgates=sl('u', 'zc'), a_ln_g=row(p['a_ln_g']), a_ln_b=row(p['a_ln_b']),
        wgl=sl('gl'), wbr=_bf(p['w_br']), wout=_bf(p['w_out']),
        r_ln_g=row(p['r_ln_g']), r_ln_b=row(p['r_ln_b']),
        subln_g=row(p['d_subln_g'] * (1.0 - lam_init)),
        a_ws=_bf(p['a_ws']),
        a_bias=jnp.repeat(jnp.swapaxes(p['a_bs'], 0, 1), A_GW, axis=1).astype(jnp.float32),
    )


def _stream(x, gs, sh, gate, lw, p, rope, lam, ctx_kv, s0, need_out):
    b_, t_, d = x.shape
    n = b_ * t_
    x2 = x.reshape(n, d)
    k_att, v_att, q = _qkv_proj(x2, t_, gs, sh, lw, rope)
    kk, v, r, kd, lwd, a, bonus = _rwkv_in_proj(x2, t_, gs, sh, lw)
    as3 = lambda z: z.reshape(b_, t_, BR_W)
    as4 = lambda z: z.reshape(N_DIR, b_, t_, BR_W)
    k_att, v_att = as3(k_att), as3(v_att)
    y0, y1, s_fin = _rwkv_scan(as4(lwd), as4(a), as4(kd), as3(kk), as3(v), as3(r), s0)
    if not need_out:
        return None, (k_att, v_att), s_fin
    ya, szb, szc = _gates_proj(x2, t_, gs, sh, lw)
    ks = [k_att] + ([ctx_kv[0]] if ctx_kv is not None else [])
    vs = [v_att] + ([ctx_kv[1]] if ctx_kv is not None else [])
    yb = _diff_attention(as3(q), as3(szb), lw['subln_g'], ks, vs, lam).reshape(n, BR_W)
    out = _merge(x2, t_, gs, sh, gate, ya, yb, y0.reshape(n, BR_W), y1.reshape(n, BR_W), bonus, szc, lw)
    return out.reshape(b_, t_, d), (k_att, v_att), s_fin


def _layer(x, xc, c_act, cc_act, rope, lam_init, p, update_ctx):
    d = D_MODEL
    b_ = x.shape[0]
    lw = _layer_tables(p, lam_init)
    mod_all = _mod_rows(jnp.concatenate([c_act, cc_act[None]], axis=0), p['w_mod'], p['b_mod'])
    mod, mod_c = mod_all[:b_], mod_all[b_:]
    rows3 = lambda z: z[:, None, :]
    g = p['norm_g']
    lp = p['d_lam']
    lam = jnp.exp(jnp.sum(lp[0] * lp[1])) - jnp.exp(jnp.sum(lp[2] * lp[3])) + lam_init
    s_zero = jnp.zeros((N_DIR, b_, N_PAIR, PAIR_W, PAIR_W), jnp.float32)

    xc_next, ctx_kv, s_ctx = _stream(xc, rows3(g * (1.0 + mod_c[:, d:2 * d])), rows3(mod_c[:, :d]),
                                     rows3(mod_c[:, 2 * d:]), lw, p, None, lam, None, s_zero, update_ctx)
    x_next, _, _ = _stream(x, rows3(g * (1.0 + mod[:, d:2 * d])), rows3(mod[:, :d]), rows3(mod[:, 2 * d:]),
                           lw, p, rope, lam, ctx_kv, s_ctx, True)
    return x_next, xc_next


def kernel(x, c, ctx, c_ctx, w_mod, b_mod, norm_g, w_in, a_ln_g, a_ln_b, a_ws, a_bs, d_qnorm, d_knorm, d_lam,
           d_subln_g, r_conv, r_w0, r_w2, r_a0, r_a2, r_kk, r_ka, r_rk, r_ln_g, r_ln_b, w_br, w_out):
    rope = _rope_tables(x.shape[1])
    c_act = jax.nn.silu(c)
    cc_act = jax.nn.silu(c_ctx)
    xc = ctx
    depth = w_in.shape[0]
    for l in range(depth):
        p = dict(w_mod=w_mod[l], b_mod=b_mod[l], norm_g=norm_g[l], w_in=w_in[l],
                 a_ln_g=a_ln_g[l], a_ln_b=a_ln_b[l], a_ws=a_ws[l], a_bs=a_bs[l],
                 d_qnorm=d_qnorm[l], d_knorm=d_knorm[l], d_lam=d_lam[l], d_subln_g=d_subln_g[l],
                 r_conv=r_conv[l], r_w0=r_w0[l], r_w2=r_w2[l], r_a0=r_a0[l], r_a2=r_a2[l],
                 r_kk=r_kk[l], r_ka=r_ka[l], r_rk=r_rk[l], r_ln_g=r_ln_g[l], r_ln_b=r_ln_b[l],
                 w_br=w_br[l], w_out=w_out[l])
        lam_init = 0.8 - 0.6 * math.exp(-0.3 * l)
        x, xc = _layer(x, xc, c_act, cc_act, rope, lam_init, p, l < depth - 1)
    return x
```

```python
import functools
import math

import jax
import jax.numpy as jnp
from jax import lax
from jax.experimental import pallas as pl
from jax.experimental.pallas import tpu as pltpu

D_MODEL = 1024
GRID_W = 64
N_BRANCH = 3
BR_W = D_MODEL // 2
A_GROUPS = 4
A_GW = BR_W // A_GROUPS
A_CHUNK = 128
B_HD = 64
B_VD = 2 * B_HD
B_HEADS = BR_W // B_VD
ROPE_BASE = 10000.0
C_HD = 64
C_HEADS = BR_W // C_HD
N_DIR = 2
W_LORA = 64
A_LORA = 64
NORM_EPS = 1e-6
LN_EPS = 1e-5
GN_EPS = 64e-5
STATE_SIZES = (BR_W, BR_W, BR_W, BR_W, N_DIR * W_LORA, N_DIR * A_LORA)
OUT_SIZES = (BR_W,) * 7 + (N_BRANCH * D_MODEL,)
STATE_COLS = sum(STATE_SIZES)
OUT_COLS = sum(OUT_SIZES)

ROW_TILE = 512
HALO_ROWS = 16
SCAN_CHUNK = 64
SCAN_CHUNKS_PER_STEP = 2
ATTN_TQ = 512
ATTN_ROWS = 256
ATTN_UNROLL = 8
ATTN_SAFE_BOUND = 60.0
VMEM_LIMIT_BYTES = 48 * 1024 * 1024
LOG2E = 1.4426950408889634
DECAY_SCALE = math.exp(-0.5)


def _tile(n, pref):
    if n <= pref:
        return n
    t = pref
    while n % t:
        t //= 2
    return t


def _bf(x):
    return x.astype(jnp.bfloat16)


def _dot(a, b):
    return jnp.dot(_bf(a), _bf(b), preferred_element_type=jnp.float32)


def _dot_nt(a, b):
    return lax.dot_general(_bf(a), _bf(b), (((1,), (1,)), ((), ())), preferred_element_type=jnp.float32)


def _split(x):
    hi = _bf(x)
    return hi, _bf(x - hi.astype(jnp.float32))


def _group_sum(x, ones_ref, exact=False):
    g = ones_ref[...]
    f = lambda a: jnp.dot(a, g, preferred_element_type=jnp.float32)
    blocks = []
    for c0 in range(0, x.shape[1], PAIR_W):
        xb = x[:, c0:c0 + PAIR_W]
        if exact:
            hi, lo = _split(xb)
            blocks.append(f(hi) + f(lo))
        else:
            blocks.append(f(_bf(xb)))
    return jnp.concatenate(blocks, axis=1)


def _params(*semantics):
    return pltpu.CompilerParams(dimension_semantics=semantics, vmem_limit_bytes=VMEM_LIMIT_BYTES)


def _mod_kernel(a_ref, w_ref, b_ref, o_ref):
    o_ref[...] = jnp.dot(a_ref[...], w_ref[...], preferred_element_type=jnp.float32) + b_ref[...]


def _mod_rows(act, w_mod, b_mod):
    r = act.shape[0]
    a = _bf(jnp.pad(act, ((0, -r % 16), (0, 0))))
    w = _bf(w_mod)
    b = b_mod.reshape(1, -1)
    out = pl.pallas_call(
        _mod_kernel,
        grid=(1,),
        in_specs=[_const_spec(a), _const_spec(w), _const_spec(b)],
        out_specs=pl.BlockSpec((a.shape[0], w.shape[1]), lambda i: (0, 0)),
        out_shape=jax.ShapeDtypeStruct((a.shape[0], w.shape[1]), jnp.float32),
        compiler_params=_params("arbitrary"), name="mod_rows",
    )(a, w, b)
    return out[:r]


def _modulated(x, gs, sh):
    rinv = lax.rsqrt(jnp.mean(x * x, axis=-1, keepdims=True) + NORM_EPS)
    return _bf(x * rinv * gs + sh)


def _qkv_kernel(x_ref, gs_ref, sh_ref, wk_ref, wv_ref, wq_ref, ones_ref, gk_ref, gq_ref, *rest, rope):
    if rope:
        cos_ref, sin_ref, k_out, v_out, q_out = rest
    else:
        k_out, v_out, q_out = rest
    h = _modulated(x_ref[...], gs_ref[...], sh_ref[...])

    if rope:
        cos = jnp.concatenate([cos_ref[...]] * B_HEADS, axis=1)
        sin = jnp.concatenate([sin_ref[...]] * B_HEADS, axis=1)
        lane = lax.broadcasted_iota(jnp.int32, (1, BR_W), 1)
        first = (lane % (B_HD // 2)) < (B_HD // 4)

    def normed(z, g_ref):
        ss = _group_sum(z * z, ones_ref)
        y = z * lax.rsqrt(ss * (1.0 / B_HD) + NORM_EPS) * g_ref[...]
        if rope:
            partner = jnp.where(first, pltpu.roll(y, BR_W - B_HD // 4, 1), pltpu.roll(y, B_HD // 4, 1))
            y = y * cos + partner * sin
        return y

    zk = jnp.dot(h, wk_ref[...], preferred_element_type=jnp.float32)
    zq = jnp.dot(h, wq_ref[...], preferred_element_type=jnp.float32)
    v_out[...] = _bf(jnp.dot(h, wv_ref[...], preferred_element_type=jnp.float32))
    k_out[...] = _bf(normed(zk, gk_ref))
    q_out[...] = _bf(normed(zq, gq_ref) * (B_HD ** -0.5 * LOG2E))


def _rwkv_in_kernel(x_ref, xp_ref, xn_ref, gs_ref, sh_ref, w_ref, wla_ref, w2_ref, a2_ref, w0_ref, a0_ref,
                    taps_ref, rkk_ref, rka_ref, rrk_ref, ones_ref,
                    kk_out, v_out, r_out, kd_out, lw_out, a_out, bonus_out, *, per_batch):
    i = pl.program_id(0)
    gs, sh = gs_ref[...], sh_ref[...]
    h = _modulated(x_ref[...], gs, sh)
    w = w_ref[...]
    zp = jnp.dot(_modulated(xp_ref[...], gs, sh), w, preferred_element_type=jnp.float32)[HALO_ROWS - 1:HALO_ROWS]
    zn = jnp.dot(_modulated(xn_ref[...], gs, sh), w, preferred_element_type=jnp.float32)[0:1]
    zp = jnp.where(i % per_batch == 0, 0.0, zp)
    zn = jnp.where(i % per_batch == per_batch - 1, 0.0, zn)
    la = jnp.dot(h, wla_ref[...], preferred_element_type=jnp.float32)
    n_l = N_DIR * W_LORA
    wl = _bf(jnp.tanh(la[:, :n_l]))
    al = _bf(la[:, n_l:])
    w_log = [w0_ref[d] + jnp.dot(wl, w2_ref[d], preferred_element_type=jnp.float32) for d in range(N_DIR)]
    a_log = [a0_ref[d] + jnp.dot(al, a2_ref[d], preferred_element_type=jnp.float32) for d in range(N_DIR)]
    z = jnp.dot(h, w, preferred_element_type=jnp.float32)
    tm = z.shape[0]
    rowi = lax.broadcasted_iota(jnp.int32, (tm, 1), 0)
    before = jnp.where(rowi == 0, zp, pltpu.roll(z, 1, 0))
    after = jnp.where(rowi == tm - 1, zn, pltpu.roll(z, tm - 1, 0))
    z = before * taps_ref[0] + z * taps_ref[1] + after * taps_ref[2]
    k, v, r = z[:, :BR_W], z[:, BR_W:2 * BR_W], z[:, 2 * BR_W:]
    v_out[...] = v
    r_out[...] = r
    kk = k * rkk_ref[...]
    kk_out[...] = kk / jnp.maximum(jnp.sqrt(_group_sum(kk * kk, ones_ref)), 1e-12)
    bonus_out[...] = _group_sum(r * k * rrk_ref[...], ones_ref) * v

    for d in range(N_DIR):
        lw_out[d] = -DECAY_SCALE * jax.nn.sigmoid(w_log[d])
        a = jax.nn.sigmoid(a_log[d])
        a_out[d] = a
        kd_out[d] = k * (1.0 + (a - 1.0) * rka_ref[...])


def _gates_kernel(x_ref, gs_ref, sh_ref, w_ref, lng_ref, lnb_ref, ws_ref, bias_ref, ya_out, szb_out, szc_out):
    h = _modulated(x_ref[...], gs_ref[...], sh_ref[...])
    z = jnp.dot(h, w_ref[...], preferred_element_type=jnp.float32)
    u, va, za, zb, zc = (z[:, i * BR_W:(i + 1) * BR_W] for i in range(5))
    szb_out[...] = _bf(jax.nn.silu(zb))
    szc_out[...] = _bf(jax.nn.silu(zc))
    uz = jax.nn.gelu(u) * jax.nn.silu(za)
    va = jax.nn.gelu(va)
    mu = jnp.mean(va, axis=-1, keepdims=True)
    var = jnp.mean(jnp.square(va - mu), axis=-1, keepdims=True)
    vn = _bf((va - mu) * lax.rsqrt(var + LN_EPS) * lng_ref[...] + lnb_ref[...])
    for r0 in range(0, vn.shape[0], A_CHUNK):
        for g in range(A_GROUPS):
            c0 = g * A_GW
            s = jnp.dot(ws_ref[g], vn[r0:r0 + A_CHUNK, c0:c0 + A_GW], preferred_element_type=jnp.float32)
            ya_out[r0:r0 + A_CHUNK, c0:c0 + A_GW] = _bf(uz[r0:r0 + A_CHUNK, c0:c0 + A_GW]
                                                        * (s + bias_ref[:, c0:c0 + A_GW]))


def _row_specs(n, t_, n_mod):
    tm = _tile(t_, ROW_TILE)
    per_batch = t_ // tm
    x_spec = pl.BlockSpec((tm, D_MODEL), lambda i: (i, 0))
    mod_spec = pl.BlockSpec((None, 1, D_MODEL), (lambda i: (i // per_batch, 0, 0)) if n_mod > 1 else (lambda i: (0, 0, 0)))
    return tm, per_batch, x_spec, mod_spec


def _const_spec(a):
    nd = a.ndim
    return pl.BlockSpec(a.shape, lambda i: (0,) * nd)


def _out_spec(tm, w):
    return pl.BlockSpec((tm, w), lambda i: (i, 0))


def _qkv_proj(x2, t_, gs, sh, lw, rope):
    n = x2.shape[0]
    tm, per_batch, x_spec, mod_spec = _row_specs(n, t_, gs.shape[0])
    consts = [lw['wk'], lw['wv'], lw['wq'], lw['ones64'], lw['gk'], lw['gq']]
    in_specs = [x_spec, mod_spec, mod_spec] + [_const_spec(a) for a in consts]
    args = [x2, gs, sh] + consts
    if rope is not None:
        rope_spec = pl.BlockSpec((tm, B_VD), lambda i: (i % per_batch, 0))
        in_specs += [rope_spec, rope_spec]
        args += list(rope)
    out = jax.ShapeDtypeStruct((n, BR_W), jnp.bfloat16)
    return pl.pallas_call(
        functools.partial(_qkv_kernel, rope=rope is not None),
        grid=(n // tm,), in_specs=in_specs, out_specs=[_out_spec(tm, BR_W)] * 3, out_shape=[out] * 3,
        compiler_params=_params("parallel"), name="qkv_proj",
    )(*args)


def _rwkv_in_proj(x2, t_, gs, sh, lw):
    n = x2.shape[0]
    tm, per_batch, x_spec, mod_spec = _row_specs(n, t_, gs.shape[0])
    halo_per_tile = tm // HALO_ROWS
    prev_spec = pl.BlockSpec((HALO_ROWS, D_MODEL), lambda i: (jnp.maximum(i * halo_per_tile - 1, 0), 0))
    next_spec = pl.BlockSpec((HALO_ROWS, D_MODEL),
                             lambda i: (jnp.minimum((i + 1) * halo_per_tile, n // HALO_ROWS - 1), 0))
    consts = [lw['wkvr'], lw['wla'], lw['w2'], lw['a2'], lw['w0'], lw['a0'], lw['taps'], lw['r_kk'], lw['r_ka'],
              lw['r_rk'], lw['ones64']]
    f32 = jax.ShapeDtypeStruct((n, BR_W), jnp.float32)
    f32d = jax.ShapeDtypeStruct((N_DIR, n, BR_W), jnp.float32)
    row_spec = _out_spec(tm, BR_W)
    dir_spec = pl.BlockSpec((N_DIR, tm, BR_W), lambda i: (0, i, 0))
    return pl.pallas_call(
        functools.partial(_rwkv_in_kernel, per_batch=per_batch),
        grid=(n // tm,),
        in_specs=[x_spec, prev_spec, next_spec, mod_spec, mod_spec] + [_const_spec(a) for a in consts],
        out_specs=[row_spec] * 3 + [dir_spec] * 3 + [row_spec],
        out_shape=[f32] * 3 + [f32d] * 3 + [f32],
        compiler_params=_params("parallel"), name="rwkv_in_proj",
    )(x2, x2, x2, gs, sh, *consts)


def _gates_proj(x2, t_, gs, sh, lw):
    n = x2.shape[0]
    tm, _, x_spec, mod_spec = _row_specs(n, t_, gs.shape[0])
    consts = [lw['wgates'], lw['a_ln_g'], lw['a_ln_b'], lw['a_ws'], lw['a_bias']]
    out = jax.ShapeDtypeStruct((n, BR_W), jnp.bfloat16)
    return pl.pallas_call(
        _gates_kernel,
        grid=(n // tm,), in_specs=[x_spec, mod_spec, mod_spec] + [_const_spec(a) for a in consts],
        out_specs=[_out_spec(tm, BR_W)] * 3, out_shape=[out] * 3,
        compiler_params=_params("parallel"), name="gates_proj",
    )(x2, gs, sh, *consts)


def _merge_kernel(x_ref, gs_ref, sh_ref, gate_ref, ya_ref, yb_ref, y0_ref, y1_ref, bonus_ref, szc_ref,
                  mean_ref, lng_ref, lnb_ref, wgl_ref, wbr_ref, wout_ref, o_ref):
    x = x_ref[...]
    y = y0_ref[...] + y1_ref[...]
    dev = y - _group_sum(y, mean_ref, exact=True)
    var = _group_sum(dev * dev, mean_ref)
    yc = (dev * lax.rsqrt(var + GN_EPS) * lng_ref[...] + lnb_ref[...] + bonus_ref[...]) * szc_ref[...].astype(jnp.float32)
    h = _modulated(x, gs_ref[...], sh_ref[...])
    g = jax.nn.sigmoid(jnp.dot(h, wgl_ref[...], preferred_element_type=jnp.float32))
    ys = (ya_ref[...], yb_ref[...], _bf(yc))
    mix = None
    for i in range(N_BRANCH):
        up = jnp.dot(ys[i], wbr_ref[i], preferred_element_type=jnp.float32)
        term = g[:, i * D_MODEL:(i + 1) * D_MODEL] * up
        mix = term if mix is None else mix + term
    o_ref[...] = x + gate_ref[...] * jnp.dot(_bf(mix), wout_ref[...], preferred_element_type=jnp.float32)


def _merge(x2, t_, gs, sh, gate, ya, yb, y0, y1, bonus, szc, lw):
    n = x2.shape[0]
    tm, _, x_spec, mod_spec = _row_specs(n, t_, gs.shape[0])
    consts = [lw['mean64'], lw['r_ln_g'], lw['r_ln_b'], lw['wgl'], lw['wbr'], lw['wout']]
    br = _out_spec(tm, BR_W)
    return pl.pallas_call(
        _merge_kernel,
        grid=(n // tm,),
        in_specs=[x_spec, mod_spec, mod_spec, mod_spec] + [br] * 6 + [_const_spec(a) for a in consts],
        out_specs=_out_spec(tm, D_MODEL), out_shape=jax.ShapeDtypeStruct((n, D_MODEL), jnp.float32),
        compiler_params=_params("parallel"), name="merge_out",
    )(x2, gs, sh, gate, ya, yb, y0, y1, bonus, szc, *consts)


def _attn_kernel(lam_ref, q_ref, szb_ref, g_ref, *refs, chunks):
    n_src = len(chunks)
    k_refs = refs[:n_src]
    v_refs = refs[n_src:2 * n_src]
    o_ref = refs[2 * n_src]
    kmax_scr = refs[2 * n_src + 1]
    q = q_ref[0]
    tq = q.shape[0]
    rows = min(tq, ATTN_ROWS)
    lane = lax.broadcasted_iota(jnp.int32, (1, B_VD), 1)
    sub = (lane < B_HD, lane >= B_HD)
    zero = jnp.zeros_like(q)
    q_sub = [jnp.where(mask, q, zero) for mask in sub]
    qs = [qj[r:r + rows] for r in range(0, tq, rows) for qj in q_sub]

    @pl.when(pl.program_id(2) == 0)
    def _():
        for j, mask in enumerate(sub):
            best = jnp.zeros((1, 1), jnp.float32)
            for k_ref in k_refs:
                kf = k_ref[0].astype(jnp.float32)
                norm2 = jnp.sum(jnp.where(mask, kf * kf, 0.0), axis=-1, keepdims=True)
                best = jnp.maximum(best, jnp.max(norm2, axis=0, keepdims=True))
            kmax_scr[j] = jnp.broadcast_to(best, kmax_scr.shape[1:])

    qf = q.astype(jnp.float32)
    bound_sub = [jnp.sqrt(jnp.sum(jnp.where(mask, qf * qf, 0.0), axis=-1, keepdims=True) * kmax_scr[j][0:1, 0:1])
                 for j, mask in enumerate(sub)]
    bounds = [bj[r:r + rows] for r in range(0, tq, rows) for bj in bound_sub]
    worst = jnp.max(jnp.maximum(bound_sub[0], bound_sub[1]))

    def scores(k):
        return [lax.dot_general(qj, k, (((1,), (1,)), ((), ())), preferred_element_type=jnp.float32) for qj in qs]

    def fixed_update(carry, k, v):
        ps = [jnp.exp2(s - b) for s, b in zip(scores(k), bounds)]
        pv = [jnp.dot(p.astype(v.dtype), v, preferred_element_type=jnp.float32) for p in ps]
        return tuple((l + jnp.sum(p, axis=-1, keepdims=True), acc + o) for (l, acc), p, o in zip(carry, ps, pv))

    def online_update(carry, k, v):
        ss = scores(k)
        m_new = [jnp.maximum(m, jnp.max(s, axis=-1, keepdims=True)) for s, (m, _, _) in zip(ss, carry)]
        ps = [jnp.exp2(s - mn) for s, mn in zip(ss, m_new)]
        pv = [jnp.dot(p.astype(v.dtype), v, preferred_element_type=jnp.float32) for p in ps]
        new = []
        for j, (m, l, acc) in enumerate(carry):
            alpha = jnp.exp2(m - m_new[j])
            new.append((m_new[j], alpha * l + jnp.sum(ps[j], axis=-1, keepdims=True), alpha * acc + pv[j]))
        return tuple(new)

    def over_keys(update, carry):
        for k_ref, v_ref, (n_chunk, tk) in zip(k_refs, v_refs, chunks):
            def body(i, c, k_ref=k_ref, v_ref=v_ref, tk=tk):
                off = pl.multiple_of(i * tk, tk)
                return update(c, k_ref[0, pl.ds(off, tk), :], v_ref[0, pl.ds(off, tk), :])
            carry = lax.fori_loop(0, n_chunk, body, carry, unroll=math.gcd(n_chunk, ATTN_UNROLL))
        return carry

    def finish(sums):
        lam = lam_ref[0]
        for b in range(tq // rows):
            (l1, acc1), (l2, acc2) = sums[2 * b], sums[2 * b + 1]
            att = acc1 / l1 - lam * (acc2 / l2)
            att = att * lax.rsqrt(jnp.mean(att * att, axis=-1, keepdims=True) + NORM_EPS) * g_ref[...]
            o_ref[0, b * rows:(b + 1) * rows, :] = _bf(att * szb_ref[0, b * rows:(b + 1) * rows, :].astype(jnp.float32))

    zeros = lambda w: jnp.zeros((rows, w), jnp.float32)

    @pl.when(worst <= ATTN_SAFE_BOUND)
    def _():
        finish(over_keys(fixed_update, tuple((zeros(1), zeros(B_VD)) for _ in qs)))

    @pl.when(worst > ATTN_SAFE_BOUND)
    def _():
        carry = over_keys(online_update, tuple((jnp.full((rows, 1), -jnp.inf, jnp.float32), zeros(1), zeros(B_VD))
                                               for _ in qs))
        finish([(l, acc) for _, l, acc in carry])


def _diff_attention(q, szb, subln_g, ks, vs, lam):
    b_, t_, _ = q.shape
    tq = _tile(t_, ATTN_TQ)
    chunks = tuple((k.shape[1] // _tile(k.shape[1], 512), _tile(k.shape[1], 512)) for k in ks)
    kv_specs = [pl.BlockSpec((1, k.shape[1], B_VD), lambda b, h, i: (b, 0, h)) for k in ks]
    q_spec = pl.BlockSpec((1, tq, B_VD), lambda b, h, i: (b, i, h))
    return pl.pallas_call(
        functools.partial(_attn_kernel, chunks=chunks),
        grid=(b_, B_HEADS, t_ // tq),
        in_specs=[pl.BlockSpec(memory_space=pltpu.SMEM), q_spec, q_spec,
                  pl.BlockSpec((1, B_VD), lambda b, h, i: (0, 0))] + kv_specs + kv_specs,
        out_specs=q_spec,
        out_shape=jax.ShapeDtypeStruct((b_, t_, BR_W), jnp.bfloat16),
        scratch_shapes=[pltpu.VMEM((2, 8, B_VD), jnp.float32)],
        compiler_params=_params("parallel", "parallel", "arbitrary"),
        name="diff_attention",
    )(lam.reshape(1).astype(jnp.float32), q, szb, subln_g, *ks, *vs)


N_PAIR = C_HEADS // 2
PAIR_W = 2 * C_HD


def _scan_kernel(lw0_ref, lw1_ref, a0_ref, a1_ref, kd0_ref, kd1_ref, kk0_ref, kk1_ref, v0_ref, v1_ref,
                 r0_ref, r1_ref, s0_ref, y0_ref, y1_ref, s_ref, *, chunk):
    @pl.when(pl.program_id(1) == 0)
    def _():
        s_ref[...] = s0_ref[...]

    ck = chunk
    n_sub = lw0_ref.shape[0] // ck
    n2 = 2 * ck
    row = lax.broadcasted_iota(jnp.int32, (n2, n2), 0)
    col = lax.broadcasted_iota(jnp.int32, (n2, n2), 1)
    gap = row % ck - col % ck
    strict = (gap > 0, gap < 0)
    incl = (gap >= 0, gap <= 0)
    eye = (row == col).astype(jnp.float32)
    gap1 = lax.broadcasted_iota(jnp.int32, (ck, ck), 0) - lax.broadcasted_iota(jnp.int32, (ck, ck), 1)
    incl1 = (_bf((gap1 >= 0).astype(jnp.float32)), _bf((gap1 <= 0).astype(jnp.float32)))
    head0 = lax.broadcasted_iota(jnp.int32, (1, PAIR_W), 1) < C_HD

    def stack(x):
        return jnp.concatenate([jnp.where(head0, x, 0.0), jnp.where(head0, 0.0, x)], axis=0)

    def scaled(d, rows, lw_ref, a_ref, kd_ref, kk_ref, v_ref, r_ref):
        lw = lw_ref[rows, :]
        lw_hi, lw_lo = _split(lw)
        cum = (jnp.dot(incl1[d], lw_hi, preferred_element_type=jnp.float32)
               + jnp.dot(incl1[d], lw_lo, preferred_element_type=jnp.float32))
        p_in = jnp.exp(cum)
        p_inv = jnp.exp(-cum)
        kk = kk_ref[rows, :]
        return dict(at=-kk * jnp.exp(cum - lw), bt=kk * a_ref[rows, :] * p_inv, kt=kd_ref[rows, :] * p_inv,
                    rt=r_ref[rows, :] * p_in, v=v_ref[rows, :], p_tot=jnp.exp(jnp.sum(lw, axis=0, keepdims=True)))

    in_refs = ((lw0_ref, a0_ref, kd0_ref, kk0_ref, v0_ref, r0_ref), (lw1_ref, a1_ref, kd1_ref, kk1_ref, v1_ref, r1_ref))
    y_refs = (y0_ref, y1_ref)
    local = lambda d, t: t if d == 0 else n_sub - 1 - t
    rows_of = lambda c: slice(c * ck, (c + 1) * ck)
    q = {(d, t): scaled(d, rows_of(local(d, t)), *in_refs[d]) for t in range(n_sub) for d in range(N_DIR)}

    chains = [(d, p, t) for t in range(n_sub) for p in range(N_PAIR) for d in range(N_DIR)]
    ids = range(len(chains))
    sl = [slice(p * PAIR_W, (p + 1) * PAIR_W) for _, p, _ in chains]
    atx, btx, ktx, rtx, vx = ([stack(q[d, t][name][:, sl[i]]) for i, (d, _, t) in enumerate(chains)]
                              for name in ('at', 'bt', 'kt', 'rt', 'v'))
    big = [_dot_nt(jnp.concatenate([atx[i], rtx[i]], axis=0), jnp.concatenate([btx[i], ktx[i]], axis=0))
           for i in ids]
    a_ab = [jnp.where(strict[chains[i][0]], big[i][:n2, :n2], 0.0) for i in ids]
    a_ak = [jnp.where(strict[chains[i][0]], big[i][:n2, n2:], 0.0) for i in ids]
    a_rb = [jnp.where(incl[chains[i][0]], big[i][n2:, :n2], 0.0) for i in ids]
    a_rk = [jnp.where(incl[chains[i][0]], big[i][n2:, n2:], 0.0) for i in ids]

    tinv = [eye + a_ab[i] for i in ids]
    apow = [_dot(a_ab[i], a_ab[i]) for i in ids]
    m = 2
    while 2 * m < ck:
        if 4 * m < ck:
            both = [_dot(apow[i], jnp.concatenate([apow[i], tinv[i]], axis=1)) for i in ids]
            apow = [both[i][:, :n2] for i in ids]
            tinv = [tinv[i] + both[i][:, n2:] for i in ids]
        else:
            tinv = [tinv[i] + _dot(apow[i], tinv[i]) for i in ids]
        m *= 2
    resid = [(eye - tinv[i]) + _dot(a_ab[i], tinv[i]) for i in ids]
    tinv = [tinv[i] + _dot(tinv[i], resid[i]) for i in ids]

    av = [_dot(jnp.concatenate([a_ak[i], a_rk[i]], axis=0), vx[i]) for i in ids]
    wu = [_dot(tinv[i], jnp.concatenate([atx[i], av[i][:n2]], axis=1)) for i in ids]
    vxt = [jnp.transpose(vx[i]) for i in ids]
    bk = [jnp.concatenate([btx[i], ktx[i]], axis=0) for i in ids]

    s = {(d, p): s_ref[d, p] for d in range(N_DIR) for p in range(N_PAIR)}
    for t in range(n_sub):
        now = [i for i in ids if chains[i][2] == t]
        st = {i: jnp.transpose(s[chains[i][:2]]) for i in now}
        u = {i: _dot(wu[i][:, :PAIR_W], st[i]) + wu[i][:, PAIR_W:] for i in now}
        yx = {i: _dot(jnp.concatenate([rtx[i], a_rb[i]], axis=1), jnp.concatenate([st[i], u[i]], axis=0))
              + av[i][n2:] for i in now}
        ds = {i: _dot(jnp.concatenate([jnp.transpose(u[i]), vxt[i]], axis=1), bk[i]) for i in now}
        for i in now:
            d, p, _ = chains[i]
            s[d, p] = (s[d, p] + ds[i]) * q[d, t]['p_tot'][:, sl[i]]
            y_refs[d][rows_of(local(d, t)), sl[i]] = yx[i][:ck] + yx[i][ck:]
    for (d, p), val in s.items():
        s_ref[d, p] = val


def _rwkv_scan(lw, a, kd, kk, v, r, s0):
    _, b_, t_, w_ = lw.shape
    ck = _tile(t_, SCAN_CHUNK)
    rows = ck * math.gcd(t_ // ck, SCAN_CHUNKS_PER_STEP)
    nb = t_ // rows
    fwd = pl.BlockSpec((None, None, rows, w_), lambda b, c: (0, b, c, 0))
    bwd = pl.BlockSpec((None, None, rows, w_), lambda b, c: (1, b, nb - 1 - c, 0))
    fwd_shared = pl.BlockSpec((None, rows, w_), lambda b, c: (b, c, 0))
    bwd_shared = pl.BlockSpec((None, rows, w_), lambda b, c: (b, nb - 1 - c, 0))
    state_spec = pl.BlockSpec((N_DIR, None, N_PAIR, PAIR_W, PAIR_W), lambda b, c: (0, b, 0, 0, 0))
    y0, y1, s_fin = pl.pallas_call(
        functools.partial(_scan_kernel, chunk=ck),
        grid=(b_, nb),
        in_specs=[fwd, bwd, fwd, bwd, fwd, bwd, fwd_shared, bwd_shared, fwd_shared, bwd_shared,
                  fwd_shared, bwd_shared, state_spec],
        out_specs=[fwd_shared, bwd_shared, state_spec],
        out_shape=[jax.ShapeDtypeStruct((b_, t_, w_), jnp.float32), jax.ShapeDtypeStruct((b_, t_, w_), jnp.float32),
                   jax.ShapeDtypeStruct((N_DIR, b_, N_PAIR, PAIR_W, PAIR_W), jnp.float32)],
        compiler_params=_params("parallel", "arbitrary"),
        name="rwkv_scan",
    )(lw, lw, a, a, kd, kd, kk, kk, v, v, r, r, s0)
    return y0, y1, s_fin


def _rope_tables(n_tok):
    rows = n_tok // GRID_W
    row = jnp.broadcast_to(jnp.arange(rows)[:, None], (rows, GRID_W)).reshape(-1)
    col = jnp.broadcast_to(jnp.arange(GRID_W)[None, :], (rows, GRID_W)).reshape(-1)
    nf = B_HD // 4
    inv = ROPE_BASE ** (-jnp.arange(nf, dtype=jnp.float32) / nf)
    ang_r = row.astype(jnp.float32)[:, None] * inv[None, :]
    ang_c = col.astype(jnp.float32)[:, None] * inv[None, :]
    cos = jnp.concatenate([jnp.cos(ang_r)] * 2 + [jnp.cos(ang_c)] * 2, axis=1)
    sin = jnp.concatenate([-jnp.sin(ang_r), jnp.sin(ang_r), -jnp.sin(ang_c), jnp.sin(ang_c)], axis=1)
    return jnp.tile(cos, (1, 2)), jnp.tile(sin, (1, 2))


def _layer_tables(p, lam_init):
    w = _bf(p['w_in'])
    off, cols = 0, {}
    for name, size in zip(('dk', 'dv', 'kr', 'vr', 'wl', 'al', 'dq', 'r', 'u', 'va', 'za', 'zb', 'zc', 'gl'),
                          STATE_SIZES + OUT_SIZES):
        cols[name] = (off, off + size)
        off += size
    sl = lambda a, b=None: w[:, cols[a][0]:cols[b or a][1]]
    grp = jnp.arange(PAIR_W) // C_HD
    ones64 =_bf((grp[:, None] == grp[None, :]).astype(jnp.float32))
    lora_rows = jnp.arange(N_DIR * W_LORA) // W_LORA

    def padded(w2):
        stacked = jnp.concatenate([w2, w2], axis=1)
        return _bf(jnp.where((lora_rows[None, :, None] == jnp.arange(N_DIR)[:, None, None]), stacked, 0.0))

    row = lambda v: v.reshape(1, -1).astype(jnp.float32)
    return dict(
        wk=sl('dk'), wv=sl('dv'), wq=sl('dq'), ones64=ones64, mean64=_bf(ones64.astype(jnp.float32) / C_HD),
        gk=row(jnp.tile(p['d_knorm'], BR_W // B_HD)), gq=row(jnp.tile(p['d_qnorm'], BR_W // B_HD)),
        wkvr=jnp.concatenate([sl('kr'), sl('vr'), sl('r')], axis=1), wla=sl('wl', 'al'),
        taps=jnp.concatenate([p['r_conv'][1], p['r_conv'][2], p['r_conv'][0]], axis=1).reshape(3, 1, 3 * BR_W),
        r_kk=row(p['r_kk']), r_ka=row(p['r_ka']), r_rk=row(p['r_rk']),
        w2=padded(p['r_w2']), a2=padded(p['r_a2']),
        w0=p['r_w0'].reshape(N_DIR, 1, BR_W), a0=p['r_a0'].reshape(N_DIR, 1, BR_W),
        wgates=sl('u', 'zc'), a_ln_g=row(p['a_ln_g']), a_ln_b=row(p['a_ln_b']),
        wgl=sl('gl'), wbr=_bf(p['w_br']), wout=_bf(p['w_out']),
        r_ln_g=row(p['r_ln_g']), r_ln_b=row(p['r_ln_b']),
        subln_g=row(p['d_subln_g'] * (1.0 - lam_init)),
        a_ws=_bf(p['a_ws']),
        a_bias=jnp.repeat(jnp.swapaxes(p['a_bs'], 0, 1), A_GW, axis=1).astype(jnp.float32),
    )


def _stream(x, gs, sh, gate, lw, p, rope, lam, ctx_kv, s0, need_out):
    b_, t_, d = x.shape
    n = b_ * t_
    x2 = x.reshape(n, d)
    k_att, v_att, q = _qkv_proj(x2, t_, gs, sh, lw, rope)
    kk, v, r, kd, lwd, a, bonus = _rwkv_in_proj(x2, t_, gs, sh, lw)
    as3 = lambda z: z.reshape(b_, t_, BR_W)
    as4 = lambda z: z.reshape(N_DIR, b_, t_, BR_W)
    k_att, v_att = as3(k_att), as3(v_att)
    y0, y1, s_fin = _rwkv_scan(as4(lwd), as4(a), as4(kd), as3(kk), as3(v), as3(r), s0)
    if not need_out:
        return None, (k_att, v_att), s_fin
    ya, szb, szc = _gates_proj(x2, t_, gs, sh, lw)
    ks = [k_att] + ([ctx_kv[0]] if ctx_kv is not None else [])
    vs = [v_att] + ([ctx_kv[1]] if ctx_kv is not None else [])
    yb = _diff_attention(as3(q), as3(szb), lw['subln_g'], ks, vs, lam).reshape(n, BR_W)
    out = _merge(x2, t_, gs, sh, gate, ya, yb, y0.reshape(n, BR_W), y1.reshape(n, BR_W), bonus, szc, lw)
    return out.reshape(b_, t_, d), (k_att, v_att), s_fin


def _layer(x, xc, c_act, cc_act, rope, lam_init, p, update_ctx):
    d = D_MODEL
    b_ = x.shape[0]
    lw = _layer_tables(p, lam_init)
    mod_all = _mod_rows(jnp.concatenate([c_act, cc_act[None]], axis=0), p['w_mod'], p['b_mod'])
    mod, mod_c = mod_all[:b_], mod_all[b_:]
    rows3 = lambda z: z[:, None, :]
    g = p['norm_g']
    lp = p['d_lam']
    lam = jnp.exp(jnp.sum(lp[0] * lp[1])) - jnp.exp(jnp.sum(lp[2] * lp[3])) + lam_init
    s_zero = jnp.zeros((N_DIR, b_, N_PAIR, PAIR_W, PAIR_W), jnp.float32)

    xc_next, ctx_kv, s_ctx = _stream(xc, rows3(g * (1.0 + mod_c[:, d:2 * d])), rows3(mod_c[:, :d]),
                                     rows3(mod_c[:, 2 * d:]), lw, p, None, lam, None, s_zero, update_ctx)
    x_next, _, _ = _stream(x, rows3(g * (1.0 + mod[:, d:2 * d])), rows3(mod[:, :d]), rows3(mod[:, 2 * d:]),
                           lw, p, rope, lam, ctx_kv, s_ctx, True)
    return x_next, xc_next


def kernel(x, c, ctx, c_ctx, w_mod, b_mod, norm_g, w_in, a_ln_g, a_ln_b, a_ws, a_bs, d_qnorm, d_knorm, d_lam,
           d_subln_g, r_conv, r_w0, r_w2, r_a0, r_a2, r_kk, r_ka, r_rk, r_ln_g, r_ln_b, w_br, w_out):
    rope = _rope_tables(x.shape[1])
    c_act = jax.nn.silu(c)
    cc_act = jax.nn.silu(c_ctx)
    xc = ctx
    depth = w_in.shape[0]
    for l in range(depth):
        p = dict(w_mod=w_mod[l], b_mod=b_mod[l], norm_g=norm_g[l], w_in=w_in[l],
                 a_ln_g=a_ln_g[l], a_ln_b=a_ln_b[l], a_ws=a_ws[l], a_bs=a_bs[l],
                 d_qnorm=d_qnorm[l], d_knorm=d_knorm[l], d_lam=d_lam[l], d_subln_g=d_subln_g[l],
                 r_conv=r_conv[l], r_w0=r_w0[l], r_w2=r_w2[l], r_a0=r_a0[l], r_a2=r_a2[l],
                 r_kk=r_kk[l], r_ka=r_ka[l], r_rk=r_rk[l], r_ln_g=r_ln_g[l], r_ln_b=r_ln_b[l],
                 w_br=w_br[l], w_out=w_out[l])
        lam_init = 0.8 - 0.6 * math.exp(-0.3 * l)
        x, xc = _layer(x, xc, c_act, cc_act, rope, lam_init, p, l < depth - 1)
    return x
```

```python
import functools
import math

import jax
import jax.numpy as jnp
from jax import lax
from jax.experimental import pallas as pl
from jax.experimental.pallas import tpu as pltpu

D_MODEL = 1024
GRID_W = 64
N_BRANCH = 3
BR_W = D_MODEL // 2
A_GROUPS = 4
A_GW = BR_W // A_GROUPS
A_CHUNK = 128
B_HD = 64
B_VD = 2 * B_HD
B_HEADS = BR_W // B_VD
ROPE_BASE = 10000.0
C_HD = 64
C_HEADS = BR_W // C_HD
N_DIR = 2
W_LORA = 64
A_LORA = 64
NORM_EPS = 1e-6
LN_EPS = 1e-5
GN_EPS = 64e-5
STATE_SIZES = (BR_W, BR_W, BR_W, BR_W, N_DIR * W_LORA, N_DIR * A_LORA)
OUT_SIZES = (BR_W,) * 7 + (N_BRANCH * D_MODEL,)
STATE_COLS = sum(STATE_SIZES)
OUT_COLS = sum(OUT_SIZES)

ROW_TILE = 512
IN_PROJ_ROWS = 256
HALO_ROWS = 16
SCAN_CHUNK = 64
SCAN_CHUNKS_PER_STEP = 2
ATTN_TQ = 512
ATTN_ROWS = 256
ATTN_UNROLL = 8
ATTN_SAFE_BOUND = 60.0
VMEM_LIMIT_BYTES = 48 * 1024 * 1024
LOG2E = 1.4426950408889634
DECAY_SCALE = math.exp(-0.5)


def _tile(n, pref):
    if n <= pref:
        return n
    t = pref
    while n % t:
        t //= 2
    return t


def _bf(x):
    return x.astype(jnp.bfloat16)


def _dot(a, b):
    return jnp.dot(_bf(a), _bf(b), preferred_element_type=jnp.float32)


def _dot_nt(a, b):
    return lax.dot_general(_bf(a), _bf(b), (((1,), (1,)), ((), ())), preferred_element_type=jnp.float32)


def _split(x):
    hi = _bf(x)
    return hi, _bf(x - hi.astype(jnp.float32))


def _group_sum(x, ones_ref, exact=False):
    g = ones_ref[...]
    f = lambda a: jnp.dot(a, g, preferred_element_type=jnp.float32)
    blocks = []
    for c0 in range(0, x.shape[1], PAIR_W):
        xb = x[:, c0:c0 + PAIR_W]
        if exact:
            hi, lo = _split(xb)
            blocks.append(f(hi) + f(lo))
        else:
            blocks.append(f(_bf(xb)))
    return jnp.concatenate(blocks, axis=1)


def _params(*semantics):
    return pltpu.CompilerParams(dimension_semantics=semantics, vmem_limit_bytes=VMEM_LIMIT_BYTES)


def _mod_kernel(a_ref, w_ref, b_ref, o_ref):
    o_ref[...] = jnp.dot(a_ref[...], w_ref[...], preferred_element_type=jnp.float32) + b_ref[...]


def _mod_rows(act, w_mod, b_mod):
    r = act.shape[0]
    a = _bf(jnp.pad(act, ((0, -r % 16), (0, 0))))
    w = _bf(w_mod)
    b = b_mod.reshape(1, -1)
    out = pl.pallas_call(
        _mod_kernel,
        grid=(1,),
        in_specs=[_const_spec(a), _const_spec(w), _const_spec(b)],
        out_specs=pl.BlockSpec((a.shape[0], w.shape[1]), lambda i: (0, 0)),
        out_shape=jax.ShapeDtypeStruct((a.shape[0], w.shape[1]), jnp.float32),
        compiler_params=_params("arbitrary"), name="mod_rows",
    )(a, w, b)
    return out[:r]


def _modulated(x, gs, sh):
    rinv = lax.rsqrt(jnp.mean(x * x, axis=-1, keepdims=True) + NORM_EPS)
    return _bf(x * rinv * gs + sh)


def _qkv_kernel(x_ref, gs_ref, sh_ref, wk_ref, wv_ref, wq_ref, ones_ref, gk_ref, gq_ref, *rest, rope):
    if rope:
        cos_ref, sin_ref, k_out, v_out, q_out = rest
    else:
        k_out, v_out, q_out = rest
    h = _modulated(x_ref[...], gs_ref[...], sh_ref[...])

    if rope:
        cos = jnp.concatenate([cos_ref[...]] * B_HEADS, axis=1)
        sin = jnp.concatenate([sin_ref[...]] * B_HEADS, axis=1)
        lane = lax.broadcasted_iota(jnp.int32, (1, BR_W), 1)
        first = (lane % (B_HD // 2)) < (B_HD // 4)

    def normed(z, g_ref):
        ss = _group_sum(z * z, ones_ref)
        y = z * lax.rsqrt(ss * (1.0 / B_HD) + NORM_EPS) * g_ref[...]
        if rope:
            partner = jnp.where(first, pltpu.roll(y, BR_W - B_HD // 4, 1), pltpu.roll(y, B_HD // 4, 1))
            y = y * cos + partner * sin
        return y

    zk = jnp.dot(h, wk_ref[...], preferred_element_type=jnp.float32)
    zq = jnp.dot(h, wq_ref[...], preferred_element_type=jnp.float32)
    v_out[...] = _bf(jnp.dot(h, wv_ref[...], preferred_element_type=jnp.float32))
    k_out[...] = _bf(normed(zk, gk_ref))
    q_out[...] = _bf(normed(zq, gq_ref) * (B_HD ** -0.5 * LOG2E))


def _rwkv_in_kernel(x_ref, xp_ref, xn_ref, gs_ref, sh_ref, w_ref, wla_ref, w2_ref, a2_ref, w0_ref, a0_ref,
                    taps_ref, rkk_ref, rka_ref, rrk_ref, ones_ref,
                    kk_out, v_out, r_out, kd_out, lw_out, a_out, bonus_out, *, per_batch):
    i = pl.program_id(0)
    gs, sh = gs_ref[...], sh_ref[...]
    h = _modulated(x_ref[...], gs, sh)
    w = w_ref[...]
    zp = jnp.dot(_modulated(xp_ref[...], gs, sh), w, preferred_element_type=jnp.float32)[HALO_ROWS - 1:HALO_ROWS]
    zn = jnp.dot(_modulated(xn_ref[...], gs, sh), w, preferred_element_type=jnp.float32)[0:1]
    zp = jnp.where(i % per_batch == 0, 0.0, zp)
    zn = jnp.where(i % per_batch == per_batch - 1, 0.0, zn)
    la = jnp.dot(h, wla_ref[...], preferred_element_type=jnp.float32)
    n_l = N_DIR * W_LORA
    wl = _bf(jnp.tanh(la[:, :n_l]))
    al = _bf(la[:, n_l:])
    w_log = [w0_ref[d] + jnp.dot(wl, w2_ref[d], preferred_element_type=jnp.float32) for d in range(N_DIR)]
    a_log = [a0_ref[d] + jnp.dot(al, a2_ref[d], preferred_element_type=jnp.float32) for d in range(N_DIR)]
    z = jnp.dot(h, w, preferred_element_type=jnp.float32)
    tm = z.shape[0]
    rowi = lax.broadcasted_iota(jnp.int32, (tm, 1), 0)
    before = jnp.where(rowi == 0, zp, pltpu.roll(z, 1, 0))
    after = jnp.where(rowi == tm - 1, zn, pltpu.roll(z, tm - 1, 0))
    z = before * taps_ref[0] + z * taps_ref[1] + after * taps_ref[2]
    k, v, r = z[:, :BR_W], z[:, BR_W:2 * BR_W], z[:, 2 * BR_W:]
    v_out[...] = v
    r_out[...] = r
    kk = k * rkk_ref[...]
    kk_out[...] = kk / jnp.maximum(jnp.sqrt(_group_sum(kk * kk, ones_ref)), 1e-12)
    bonus_out[...] = _group_sum(r * k * rrk_ref[...], ones_ref) * v

    for d in range(N_DIR):
        lw_out[d] = -DECAY_SCALE * jax.nn.sigmoid(w_log[d])
        a = jax.nn.sigmoid(a_log[d])
        a_out[d] = a
        kd_out[d] = k * (1.0 + (a - 1.0) * rka_ref[...])


def _gates_kernel(x_ref, gs_ref, sh_ref, w_ref, lng_ref, lnb_ref, ws_ref, bias_ref, ya_out, szb_out, szc_out):
    h = _modulated(x_ref[...], gs_ref[...], sh_ref[...])
    z = jnp.dot(h, w_ref[...], preferred_element_type=jnp.float32)
    u, va, za, zb, zc = (z[:, i * BR_W:(i + 1) * BR_W] for i in range(5))
    szb_out[...] = _bf(jax.nn.silu(zb))
    szc_out[...] = _bf(jax.nn.silu(zc))
    uz = jax.nn.gelu(u) * jax.nn.silu(za)
    va = jax.nn.gelu(va)
    mu = jnp.mean(va, axis=-1, keepdims=True)
    var = jnp.mean(jnp.square(va - mu), axis=-1, keepdims=True)
    vn = _bf((va - mu) * lax.rsqrt(var + LN_EPS) * lng_ref[...] + lnb_ref[...])
    for r0 in range(0, vn.shape[0], A_CHUNK):
        for g in range(A_GROUPS):
            c0 = g * A_GW
            s = jnp.dot(ws_ref[g], vn[r0:r0 + A_CHUNK, c0:c0 + A_GW], preferred_element_type=jnp.float32)
            ya_out[r0:r0 + A_CHUNK, c0:c0 + A_GW] = _bf(uz[r0:r0 + A_CHUNK, c0:c0 + A_GW]
                                                        * (s + bias_ref[:, c0:c0 + A_GW]))


def _in_proj_kernel(*refs, rope, per_batch):
    x_ref, xp_ref, xn_ref, gs_ref, sh_ref = refs[:5]
    n_qkv = 6 + (2 if rope else 0)
    qkv_in = refs[5:5 + n_qkv]
    rwkv_in = refs[5 + n_qkv:5 + n_qkv + 11]
    gates_in = refs[5 + n_qkv + 11:5 + n_qkv + 16]
    outs = refs[5 + n_qkv + 16:]
    _qkv_kernel(x_ref, gs_ref, sh_ref, *qkv_in, *outs[:3], rope=rope)
    _rwkv_in_kernel(x_ref, xp_ref, xn_ref, gs_ref, sh_ref, *rwkv_in, *outs[3:10], per_batch=per_batch)
    _gates_kernel(x_ref, gs_ref, sh_ref, *gates_in, *outs[10:])


def _in_proj(x2, t_, gs, sh, lw, rope):
    n = x2.shape[0]
    tm, per_batch, x_spec, mod_spec = _row_specs(n, t_, gs.shape[0], IN_PROJ_ROWS)
    halo_per_tile = tm // HALO_ROWS
    prev_spec = pl.BlockSpec((HALO_ROWS, D_MODEL), lambda i: (jnp.maximum(i * halo_per_tile - 1, 0), 0))
    next_spec = pl.BlockSpec((HALO_ROWS, D_MODEL),
                             lambda i: (jnp.minimum((i + 1) * halo_per_tile, n // HALO_ROWS - 1), 0))
    qkv_c = [lw['wk'], lw['wv'], lw['wq'], lw['ones64'], lw['gk'], lw['gq']]
    rwkv_c = [lw['wkvr'], lw['wla'], lw['w2'], lw['a2'], lw['w0'], lw['a0'], lw['taps'], lw['r_kk'], lw['r_ka'],
              lw['r_rk'], lw['ones64']]
    gates_c = [lw['wgates'], lw['a_ln_g'], lw['a_ln_b'], lw['a_ws'], lw['a_bias']]
    in_specs = [x_spec, prev_spec, next_spec, mod_spec, mod_spec] + [_const_spec(a) for a in qkv_c]
    args = [x2, x2, x2, gs, sh] + qkv_c
    if rope is not None:
        rope_spec = pl.BlockSpec((tm, B_VD), lambda i: (i % per_batch, 0))
        in_specs += [rope_spec, rope_spec]
        args += list(rope)
    in_specs += [_const_spec(a) for a in rwkv_c + gates_c]
    args += rwkv_c + gates_c
    bf = jax.ShapeDtypeStruct((n, BR_W), jnp.bfloat16)
    f32 = jax.ShapeDtypeStruct((n, BR_W), jnp.float32)
    f32d = jax.ShapeDtypeStruct((N_DIR, n, BR_W), jnp.float32)
    row_spec = _out_spec(tm, BR_W)
    dir_spec = pl.BlockSpec((N_DIR, tm, BR_W), lambda i: (0, i, 0))
    return pl.pallas_call(
        functools.partial(_in_proj_kernel, rope=rope is not None, per_batch=per_batch),
        grid=(n // tm,), in_specs=in_specs,
        out_specs=[row_spec] * 3 + [row_spec] * 3 + [dir_spec] * 3 + [row_spec] + [row_spec] * 3,
        out_shape=[bf] * 3 + [f32] * 3 + [f32d] * 3 + [f32] + [bf] * 3,
        compiler_params=_params("parallel"), name="in_proj",
    )(*args)


def _row_specs(n, t_, n_mod, rows=None):
    tm = _tile(t_, rows or ROW_TILE)
    per_batch = t_ // tm
    x_spec = pl.BlockSpec((tm, D_MODEL), lambda i: (i, 0))
    mod_spec = pl.BlockSpec((None, 1, D_MODEL), (lambda i: (i // per_batch, 0, 0)) if n_mod > 1 else (lambda i: (0, 0, 0)))
    return tm, per_batch, x_spec, mod_spec


def _const_spec(a):
    nd = a.ndim
    return pl.BlockSpec(a.shape, lambda i: (0,) * nd)


def _out_spec(tm, w):
    return pl.BlockSpec((tm, w), lambda i: (i, 0))


def _qkv_proj(x2, t_, gs, sh, lw, rope):
    n = x2.shape[0]
    tm, per_batch, x_spec, mod_spec = _row_specs(n, t_, gs.shape[0])
    consts = [lw['wk'], lw['wv'], lw['wq'], lw['ones64'], lw['gk'], lw['gq']]
    in_specs = [x_spec, mod_spec, mod_spec] + [_const_spec(a) for a in consts]
    args = [x2, gs, sh] + consts
    if rope is not None:
        rope_spec = pl.BlockSpec((tm, B_VD), lambda i: (i % per_batch, 0))
        in_specs += [rope_spec, rope_spec]
        args += list(rope)
    out = jax.ShapeDtypeStruct((n, BR_W), jnp.bfloat16)
    return pl.pallas_call(
        functools.partial(_qkv_kernel, rope=rope is not None),
        grid=(n // tm,), in_specs=in_specs, out_specs=[_out_spec(tm, BR_W)] * 3, out_shape=[out] * 3,
        compiler_params=_params("parallel"), name="qkv_proj",
    )(*args)


def _rwkv_in_proj(x2, t_, gs, sh, lw):
    n = x2.shape[0]
    tm, per_batch, x_spec, mod_spec = _row_specs(n, t_, gs.shape[0])
    halo_per_tile = tm // HALO_ROWS
    prev_spec = pl.BlockSpec((HALO_ROWS, D_MODEL), lambda i: (jnp.maximum(i * halo_per_tile - 1, 0), 0))
    next_spec = pl.BlockSpec((HALO_ROWS, D_MODEL),
                             lambda i: (jnp.minimum((i + 1) * halo_per_tile, n // HALO_ROWS - 1), 0))
    consts = [lw['wkvr'], lw['wla'], lw['w2'], lw['a2'], lw['w0'], lw['a0'], lw['taps'], lw['r_kk'], lw['r_ka'],
              lw['r_rk'], lw['ones64']]
    f32 = jax.ShapeDtypeStruct((n, BR_W), jnp.float32)
    f32d = jax.ShapeDtypeStruct((N_DIR, n, BR_W), jnp.float32)
    row_spec = _out_spec(tm, BR_W)
    dir_spec = pl.BlockSpec((N_DIR, tm, BR_W), lambda i: (0, i, 0))
    return pl.pallas_call(
        functools.partial(_rwkv_in_kernel, per_batch=per_batch),
        grid=(n // tm,),
        in_specs=[x_spec, prev_spec, next_spec, mod_spec, mod_spec] + [_const_spec(a) for a in consts],
        out_specs=[row_spec] * 3 + [dir_spec] * 3 + [row_spec],
        out_shape=[f32] * 3 + [f32d] * 3 + [f32],
        compiler_params=_params("parallel"), name="rwkv_in_proj",
    )(x2, x2, x2, gs, sh, *consts)


def _gates_proj(x2, t_, gs, sh, lw):
    n = x2.shape[0]
    tm, _, x_spec, mod_spec = _row_specs(n, t_, gs.shape[0])
    consts = [lw['wgates'], lw['a_ln_g'], lw['a_ln_b'], lw['a_ws'], lw['a_bias']]
    out = jax.ShapeDtypeStruct((n, BR_W), jnp.bfloat16)
    return pl.pallas_call(
        _gates_kernel,
        grid=(n // tm,), in_specs=[x_spec, mod_spec, mod_spec] + [_const_spec(a) for a in consts],
        out_specs=[_out_spec(tm, BR_W)] * 3, out_shape=[out] * 3,
        compiler_params=_params("parallel"), name="gates_proj",
    )(x2, gs, sh, *consts)


def _merge_kernel(x_ref, gs_ref, sh_ref, gate_ref, ya_ref, yb_ref, y0_ref, y1_ref, bonus_ref, szc_ref,
                  mean_ref, lng_ref, lnb_ref, wgl_ref, wbr_ref, wout_ref, o_ref):
    x = x_ref[...]
    y = y0_ref[...] + y1_ref[...]
    dev = y - _group_sum(y, mean_ref, exact=True)
    var = _group_sum(dev * dev, mean_ref)
    yc = (dev * lax.rsqrt(var + GN_EPS) * lng_ref[...] + lnb_ref[...] + bonus_ref[...]) * szc_ref[...].astype(jnp.float32)
    h = _modulated(x, gs_ref[...], sh_ref[...])
    g = jax.nn.sigmoid(jnp.dot(h, wgl_ref[...], preferred_element_type=jnp.float32))
    ys = (ya_ref[...], yb_ref[...], _bf(yc))
    mix = None
    for i in range(N_BRANCH):
        up = jnp.dot(ys[i], wbr_ref[i], preferred_element_type=jnp.float32)
        term = g[:, i * D_MODEL:(i + 1) * D_MODEL] * up
        mix = term if mix is None else mix + term
    o_ref[...] = x + gate_ref[...] * jnp.dot(_bf(mix), wout_ref[...], preferred_element_type=jnp.float32)


def _merge(x2, t_, gs, sh, gate, ya, yb, y0, y1, bonus, szc, lw):
    n = x2.shape[0]
    tm, _, x_spec, mod_spec = _row_specs(n, t_, gs.shape[0])
    consts = [lw['mean64'], lw['r_ln_g'], lw['r_ln_b'], lw['wgl'], lw['wbr'], lw['wout']]
    br = _out_spec(tm, BR_W)
    return pl.pallas_call(
        _merge_kernel,
        grid=(n // tm,),
        in_specs=[x_spec, mod_spec, mod_spec, mod_spec] + [br] * 6 + [_const_spec(a) for a in consts],
        out_specs=_out_spec(tm, D_MODEL), out_shape=jax.ShapeDtypeStruct((n, D_MODEL), jnp.float32),
        compiler_params=_params("parallel"), name="merge_out",
    )(x2, gs, sh, gate, ya, yb, y0, y1, bonus, szc, *consts)


def _attn_kernel(lam_ref, q_ref, szb_ref, g_ref, *refs, chunks):
    n_src = len(chunks)
    k_refs = refs[:n_src]
    v_refs = refs[n_src:2 * n_src]
    o_ref = refs[2 * n_src]
    kmax_scr = refs[2 * n_src + 1]
    q = q_ref[0]
    tq = q.shape[0]
    rows = min(tq, ATTN_ROWS)
    lane = lax.broadcasted_iota(jnp.int32, (1, B_VD), 1)
    sub = (lane < B_HD, lane >= B_HD)
    zero = jnp.zeros_like(q)
    q_sub = [jnp.where(mask, q, zero) for mask in sub]
    qs = [qj[r:r + rows] for r in range(0, tq, rows) for qj in q_sub]

    @pl.when(pl.program_id(2) == 0)
    def _():
        for j, mask in enumerate(sub):
            best = jnp.zeros((1, 1), jnp.float32)
            for k_ref in k_refs:
                kf = k_ref[0].astype(jnp.float32)
                norm2 = jnp.sum(jnp.where(mask, kf * kf, 0.0), axis=-1, keepdims=True)
                best = jnp.maximum(best, jnp.max(norm2, axis=0, keepdims=True))
            kmax_scr[j] = jnp.broadcast_to(best, kmax_scr.shape[1:])

    qf = q.astype(jnp.float32)
    bound_sub = [jnp.sqrt(jnp.sum(jnp.where(mask, qf * qf, 0.0), axis=-1, keepdims=True) * kmax_scr[j][0:1, 0:1])
                 for j, mask in enumerate(sub)]
    bounds = [bj[r:r + rows] for r in range(0, tq, rows) for bj in bound_sub]
    worst = jnp.max(jnp.maximum(bound_sub[0], bound_sub[1]))

    def scores(k):
        return [lax.dot_general(qj, k, (((1,), (1,)), ((), ())), preferred_element_type=jnp.float32) for qj in qs]

    def fixed_update(carry, k, v):
        ps = [jnp.exp2(s - b) for s, b in zip(scores(k), bounds)]
        pv = [jnp.dot(p.astype(v.dtype), v, preferred_element_type=jnp.float32) for p in ps]
        return tuple((l + jnp.sum(p, axis=-1, keepdims=True), acc + o) for (l, acc), p, o in zip(carry, ps, pv))

    def online_update(carry, k, v):
        ss = scores(k)
        m_new = [jnp.maximum(m, jnp.max(s, axis=-1, keepdims=True)) for s, (m, _, _) in zip(ss, carry)]
        ps = [jnp.exp2(s - mn) for s, mn in zip(ss, m_new)]
        pv = [jnp.dot(p.astype(v.dtype), v, preferred_element_type=jnp.float32) for p in ps]
        new = []
        for j, (m, l, acc) in enumerate(carry):
            alpha = jnp.exp2(m - m_new[j])
            new.append((m_new[j], alpha * l + jnp.sum(ps[j], axis=-1, keepdims=True), alpha * acc + pv[j]))
        return tuple(new)

    def over_keys(update, carry):
        for k_ref, v_ref, (n_chunk, tk) in zip(k_refs, v_refs, chunks):
            def body(i, c, k_ref=k_ref, v_ref=v_ref, tk=tk):
                off = pl.multiple_of(i * tk, tk)
                return update(c, k_ref[0, pl.ds(off, tk), :], v_ref[0, pl.ds(off, tk), :])
            carry = lax.fori_loop(0, n_chunk, body, carry, unroll=math.gcd(n_chunk, ATTN_UNROLL))
        return carry

    def finish(sums):
        lam = lam_ref[0]
        for b in range(tq // rows):
            (l1, acc1), (l2, acc2) = sums[2 * b], sums[2 * b + 1]
            att = acc1 / l1 - lam * (acc2 / l2)
            att = att * lax.rsqrt(jnp.mean(att * att, axis=-1, keepdims=True) + NORM_EPS) * g_ref[...]
            o_ref[0, b * rows:(b + 1) * rows, :] = _bf(att * szb_ref[0, b * rows:(b + 1) * rows, :].astype(jnp.float32))

    zeros = lambda w: jnp.zeros((rows, w), jnp.float32)

    @pl.when(worst <= ATTN_SAFE_BOUND)
    def _():
        finish(over_keys(fixed_update, tuple((zeros(1), zeros(B_VD)) for _ in qs)))

    @pl.when(worst > ATTN_SAFE_BOUND)
    def _():
        carry = over_keys(online_update, tuple((jnp.full((rows, 1), -jnp.inf, jnp.float32), zeros(1), zeros(B_VD))
                                               for _ in qs))
        finish([(l, acc) for _, l, acc in carry])


def _diff_attention(q, szb, subln_g, ks, vs, lam):
    b_, t_, _ = q.shape
    tq = _tile(t_, ATTN_TQ)
    chunks = tuple((k.shape[1] // _tile(k.shape[1], 512), _tile(k.shape[1], 512)) for k in ks)
    kv_specs = [pl.BlockSpec((1, k.shape[1], B_VD), lambda b, h, i: (b, 0, h)) for k in ks]
    q_spec = pl.BlockSpec((1, tq, B_VD), lambda b, h, i: (b, i, h))
    return pl.pallas_call(
        functools.partial(_attn_kernel, chunks=chunks),
        grid=(b_, B_HEADS, t_ // tq),
        in_specs=[pl.BlockSpec(memory_space=pltpu.SMEM), q_spec, q_spec,
                  pl.BlockSpec((1, B_VD), lambda b, h, i: (0, 0))] + kv_specs + kv_specs,
        out_specs=q_spec,
        out_shape=jax.ShapeDtypeStruct((b_, t_, BR_W), jnp.bfloat16),
        scratch_shapes=[pltpu.VMEM((2, 8, B_VD), jnp.float32)],
        compiler_params=_params("parallel", "parallel", "arbitrary"),
        name="diff_attention",
    )(lam.reshape(1).astype(jnp.float32), q, szb, subln_g, *ks, *vs)


N_PAIR = C_HEADS // 2
PAIR_W = 2 * C_HD


def _scan_kernel(lw0_ref, lw1_ref, a0_ref, a1_ref, kd0_ref, kd1_ref, kk0_ref, kk1_ref, v0_ref, v1_ref,
                 r0_ref, r1_ref, s0_ref, y0_ref, y1_ref, s_ref, *, chunk):
    @pl.when(pl.program_id(1) == 0)
    def _():
        s_ref[...] = s0_ref[...]

    ck = chunk
    n_sub = lw0_ref.shape[0] // ck
    n2 = 2 * ck
    row = lax.broadcasted_iota(jnp.int32, (n2, n2), 0)
    col = lax.broadcasted_iota(jnp.int32, (n2, n2), 1)
    gap = row % ck - col % ck
    strict = (gap > 0, gap < 0)
    incl = (gap >= 0, gap <= 0)
    eye = (row == col).astype(jnp.float32)
    gap1 = lax.broadcasted_iota(jnp.int32, (ck, ck), 0) - lax.broadcasted_iota(jnp.int32, (ck, ck), 1)
    incl1 = (_bf((gap1 >= 0).astype(jnp.float32)), _bf((gap1 <= 0).astype(jnp.float32)))
    head0 = lax.broadcasted_iota(jnp.int32, (1, PAIR_W), 1) < C_HD

    def stack(x):
        return jnp.concatenate([jnp.where(head0, x, 0.0), jnp.where(head0, 0.0, x)], axis=0)

    def scaled(d, rows, lw_ref, a_ref, kd_ref, kk_ref, v_ref, r_ref):
        lw = lw_ref[rows, :]
        lw_hi, lw_lo = _split(lw)
        cum = (jnp.dot(incl1[d], lw_hi, preferred_element_type=jnp.float32)
               + jnp.dot(incl1[d], lw_lo, preferred_element_type=jnp.float32))
        p_in = jnp.exp(cum)
        p_inv = jnp.exp(-cum)
        kk = kk_ref[rows, :]
        return dict(at=-kk * jnp.exp(cum - lw), bt=kk * a_ref[rows, :] * p_inv, kt=kd_ref[rows, :] * p_inv,
                    rt=r_ref[rows, :] * p_in, v=v_ref[rows, :], p_tot=jnp.exp(jnp.sum(lw, axis=0, keepdims=True)))

    in_refs = ((lw0_ref, a0_ref, kd0_ref, kk0_ref, v0_ref, r0_ref), (lw1_ref, a1_ref, kd1_ref, kk1_ref, v1_ref, r1_ref))
    y_refs = (y0_ref, y1_ref)
    local = lambda d, t: t if d == 0 else n_sub - 1 - t
    rows_of = lambda c: slice(c * ck, (c + 1) * ck)
    q = {(d, t): scaled(d, rows_of(local(d, t)), *in_refs[d]) for t in range(n_sub) for d in range(N_DIR)}

    chains = [(d, p, t) for t in range(n_sub) for p in range(N_PAIR) for d in range(N_DIR)]
    ids = range(len(chains))
    sl = [slice(p * PAIR_W, (p + 1) * PAIR_W) for _, p, _ in chains]
    atx, btx, ktx, rtx, vx = ([stack(q[d, t][name][:, sl[i]]) for i, (d, _, t) in enumerate(chains)]
                              for name in ('at', 'bt', 'kt', 'rt', 'v'))
    big = [_dot_nt(jnp.concatenate([atx[i], rtx[i]], axis=0), jnp.concatenate([btx[i], ktx[i]], axis=0))
           for i in ids]
    a_ab = [jnp.where(strict[chains[i][0]], big[i][:n2, :n2], 0.0) for i in ids]
    a_ak = [jnp.where(strict[chains[i][0]], big[i][:n2, n2:], 0.0) for i in ids]
    a_rb = [jnp.where(incl[chains[i][0]], big[i][n2:, :n2], 0.0) for i in ids]
    a_rk = [jnp.where(incl[chains[i][0]], big[i][n2:, n2:], 0.0) for i in ids]

    tinv = [eye + a_ab[i] for i in ids]
    apow = [_dot(a_ab[i], a_ab[i]) for i in ids]
    m = 2
    while 2 * m < ck:
        if 4 * m < ck:
            both = [_dot(apow[i], jnp.concatenate([apow[i], tinv[i]], axis=1)) for i in ids]
            apow = [both[i][:, :n2] for i in ids]
            tinv = [tinv[i] + both[i][:, n2:] for i in ids]
        else:
            tinv = [tinv[i] + _dot(apow[i], tinv[i]) for i in ids]
        m *= 2
    resid = [(eye - tinv[i]) + _dot(a_ab[i], tinv[i]) for i in ids]
    tinv = [tinv[i] + _dot(tinv[i], resid[i]) for i in ids]

    av = [_dot(jnp.concatenate([a_ak[i], a_rk[i]], axis=0), vx[i]) for i in ids]
    wu = [_dot(tinv[i], jnp.concatenate([atx[i], av[i][:n2]], axis=1)) for i in ids]
    vxt = [jnp.transpose(vx[i]) for i in ids]
    bk = [jnp.concatenate([btx[i], ktx[i]], axis=0) for i in ids]

    s = {(d, p): s_ref[d, p] for d in range(N_DIR) for p in range(N_PAIR)}
    for t in range(n_sub):
        now = [i for i in ids if chains[i][2] == t]
        st = {i: jnp.transpose(s[chains[i][:2]]) for i in now}
        u = {i: _dot(wu[i][:, :PAIR_W], st[i]) + wu[i][:, PAIR_W:] for i in now}
        yx = {i: _dot(jnp.concatenate([rtx[i], a_rb[i]], axis=1), jnp.concatenate([st[i], u[i]], axis=0))
              + av[i][n2:] for i in now}
        ds = {i: _dot(jnp.concatenate([jnp.transpose(u[i]), vxt[i]], axis=1), bk[i]) for i in now}
        for i in now:
            d, p, _ = chains[i]
            s[d, p] = (s[d, p] + ds[i]) * q[d, t]['p_tot'][:, sl[i]]
            y_refs[d][rows_of(local(d, t)), sl[i]] = yx[i][:ck] + yx[i][ck:]
    for (d, p), val in s.items():
        s_ref[d, p] = val


def _rwkv_scan(lw, a, kd, kk, v, r, s0):
    _, b_, t_, w_ = lw.shape
    ck = _tile(t_, SCAN_CHUNK)
    rows = ck * math.gcd(t_ // ck, SCAN_CHUNKS_PER_STEP)
    nb = t_ // rows
    fwd = pl.BlockSpec((None, None, rows, w_), lambda b, c: (0, b, c, 0))
    bwd = pl.BlockSpec((None, None, rows, w_), lambda b, c: (1, b, nb - 1 - c, 0))
    fwd_shared = pl.BlockSpec((None, rows, w_), lambda b, c: (b, c, 0))
    bwd_shared = pl.BlockSpec((None, rows, w_), lambda b, c: (b, nb - 1 - c, 0))
    state_spec = pl.BlockSpec((N_DIR, None, N_PAIR, PAIR_W, PAIR_W), lambda b, c: (0, b, 0, 0, 0))
    y0, y1, s_fin = pl.pallas_call(
        functools.partial(_scan_kernel, chunk=ck),
        grid=(b_, nb),
        in_specs=[fwd, bwd, fwd, bwd, fwd, bwd, fwd_shared, bwd_shared, fwd_shared, bwd_shared,
                  fwd_shared, bwd_shared, state_spec],
        out_specs=[fwd_shared, bwd_shared, state_spec],
        out_shape=[jax.ShapeDtypeStruct((b_, t_, w_), jnp.float32), jax.ShapeDtypeStruct((b_, t_, w_), jnp.float32),
                   jax.ShapeDtypeStruct((N_DIR, b_, N_PAIR, PAIR_W, PAIR_W), jnp.float32)],
        compiler_params=_params("parallel", "arbitrary"),
        name="rwkv_scan",
    )(lw, lw, a, a, kd, kd, kk, kk, v, v, r, r, s0)
    return y0, y1, s_fin


def _rope_tables(n_tok):
    rows = n_tok // GRID_W
    row = jnp.broadcast_to(jnp.arange(rows)[:, None], (rows, GRID_W)).reshape(-1)
    col = jnp.broadcast_to(jnp.arange(GRID_W)[None, :], (rows, GRID_W)).reshape(-1)
    nf = B_HD // 4
    inv = ROPE_BASE ** (-jnp.arange(nf, dtype=jnp.float32) / nf)
    ang_r = row.astype(jnp.float32)[:, None] * inv[None, :]
    ang_c = col.astype(jnp.float32)[:, None] * inv[None, :]
    cos = jnp.concatenate([jnp.cos(ang_r)] * 2 + [jnp.cos(ang_c)] * 2, axis=1)
    sin = jnp.concatenate([-jnp.sin(ang_r), jnp.sin(ang_r), -jnp.sin(ang_c), jnp.sin(ang_c)], axis=1)
    return jnp.tile(cos, (1, 2)), jnp.tile(sin, (1, 2))


def _layer_tables(p, lam_init):
    w = _bf(p['w_in'])
    off, cols = 0, {}
    for name, size in zip(('dk', 'dv', 'kr', 'vr', 'wl', 'al', 'dq', 'r', 'u', 'va', 'za', 'zb', 'zc', 'gl'),
                          STATE_SIZES + OUT_SIZES):
        cols[name] = (off, off + size)
        off += size
    sl = lambda a, b=None: w[:, cols[a][0]:cols[b or a][1]]
    grp = jnp.arange(PAIR_W) // C_HD
    ones64 =_bf((grp[:, None] == grp[None, :]).astype(jnp.float32))
    lora_rows = jnp.arange(N_DIR * W_LORA) // W_LORA

    def padded(w2):
        stacked = jnp.concatenate([w2, w2], axis=1)
        return _bf(jnp.where((lora_rows[None, :, None] == jnp.arange(N_DIR)[:, None, None]), stacked, 0.0))

    row = lambda v: v.reshape(1, -1).astype(jnp.float32)
    return dict(
        wk=sl('dk'), wv=sl('dv'), wq=sl('dq'), ones64=ones64, mean64=_bf(ones64.astype(jnp.float32) / C_HD),
        gk=row(jnp.tile(p['d_knorm'], BR_W // B_HD)), gq=row(jnp.tile(p['d_qnorm'], BR_W // B_HD)),
        wkvr=jnp.concatenate([sl('kr'), sl('vr'), sl('r')], axis=1), wla=sl('wl', 'al'),
        taps=jnp.concatenate([p['r_conv'][1], p['r_conv'][2], p['r_conv'][0]], axis=1).reshape(3, 1, 3 * BR_W),
        r_kk=row(p['r_kk']), r_ka=row(p['r_ka']), r_rk=row(p['r_rk']),
        w2=padded(p['r_w2']), a2=padded(p['r_a2']),
        w0=p['r_w0'].reshape(N_DIR, 1, BR_W), a0=p['r_a0'].reshape(N_DIR, 1, BR_W),
        wgates=sl('u', 'zc'), a_ln_g=row(p['a_ln_g']), a_ln_b=row(p['a_ln_b']),
        wgl=sl('gl'), wbr=_bf(p['w_br']), wout=_bf(p['w_out']),
        r_ln_g=row(p['r_ln_g']), r_ln_b=row(p['r_ln_b']),
        subln_g=row(p['d_subln_g'] * (1.0 - lam_init)),
        a_ws=_bf(p['a_ws']),
        a_bias=jnp.repeat(jnp.swapaxes(p['a_bs'], 0, 1), A_GW, axis=1).astype(jnp.float32),
    )


def _stream(x, gs, sh, gate, lw, p, rope, lam, ctx_kv, s0, need_out):
    b_, t_, d = x.shape
    n = b_ * t_
    x2 = x.reshape(n, d)
    k_att, v_att, q, kk, v, r, kd, lwd, a, bonus, ya, szb, szc = _in_proj(x2, t_, gs, sh, lw, rope)
    as3 = lambda z: z.reshape(b_, t_, BR_W)
    as4 = lambda z: z.reshape(N_DIR, b_, t_, BR_W)
    k_att, v_att = as3(k_att), as3(v_att)
    y0, y1, s_fin = _rwkv_scan(as4(lwd), as4(a), as4(kd), as3(kk), as3(v), as3(r), s0)
    if not need_out:
        return None, (k_att, v_att), s_fin
    ks = [k_att] + ([ctx_kv[0]] if ctx_kv is not None else [])
    vs = [v_att] + ([ctx_kv[1]] if ctx_kv is not None else [])
    yb = _diff_attention(as3(q), as3(szb), lw['subln_g'], ks, vs, lam).reshape(n, BR_W)
    out = _merge(x2, t_, gs, sh, gate, ya, yb, y0.reshape(n, BR_W), y1.reshape(n, BR_W), bonus, szc, lw)
    return out.reshape(b_, t_, d), (k_att, v_att), s_fin


def _layer(x, xc, c_act, cc_act, rope, lam_init, p, update_ctx):
    d = D_MODEL
    b_ = x.shape[0]
    lw = _layer_tables(p, lam_init)
    mod_all = _mod_rows(jnp.concatenate([c_act, cc_act[None]], axis=0), p['w_mod'], p['b_mod'])
    mod, mod_c = mod_all[:b_], mod_all[b_:]
    rows3 = lambda z: z[:, None, :]
    g = p['norm_g']
    lp = p['d_lam']
    lam = jnp.exp(jnp.sum(lp[0] * lp[1])) - jnp.exp(jnp.sum(lp[2] * lp[3])) + lam_init
    s_zero = jnp.zeros((N_DIR, b_, N_PAIR, PAIR_W, PAIR_W), jnp.float32)

    xc_next, ctx_kv, s_ctx = _stream(xc, rows3(g * (1.0 + mod_c[:, d:2 * d])), rows3(mod_c[:, :d]),
                                     rows3(mod_c[:, 2 * d:]), lw, p, None, lam, None, s_zero, update_ctx)
    x_next, _, _ = _stream(x, rows3(g * (1.0 + mod[:, d:2 * d])), rows3(mod[:, :d]), rows3(mod[:, 2 * d:]),
                           lw, p, rope, lam, ctx_kv, s_ctx, True)
    return x_next, xc_next


def kernel(x, c, ctx, c_ctx, w_mod, b_mod, norm_g, w_in, a_ln_g, a_ln_b, a_ws, a_bs, d_qnorm, d_knorm, d_lam,
           d_subln_g, r_conv, r_w0, r_w2, r_a0, r_a2, r_kk, r_ka, r_rk, r_ln_g, r_ln_b, w_br, w_out):
    rope = _rope_tables(x.shape[1])
    c_act = jax.nn.silu(c)
    cc_act = jax.nn.silu(c_ctx)
    xc = ctx
    depth = w_in.shape[0]
    for l in range(depth):
        p = dict(w_mod=w_mod[l], b_mod=b_mod[l], norm_g=norm_g[l], w_in=w_in[l],
                 a_ln_g=a_ln_g[l], a_ln_b=a_ln_b[l], a_ws=a_ws[l], a_bs=a_bs[l],
                 d_qnorm=d_qnorm[l], d_knorm=d_knorm[l], d_lam=d_lam[l], d_subln_g=d_subln_g[l],
                 r_conv=r_conv[l], r_w0=r_w0[l], r_w2=r_w2[l], r_a0=r_a0[l], r_a2=r_a2[l],
                 r_kk=r_kk[l], r_ka=r_ka[l], r_rk=r_rk[l], r_ln_g=r_ln_g[l], r_ln_b=r_ln_b[l],
                 w_br=w_br[l], w_out=w_out[l])
        lam_init = 0.8 - 0.6 * math.exp(-0.3 * l)
        x, xc = _layer(x, xc, c_act, cc_act, rope, lam_init, p, l < depth - 1)
    return x
```
